```python
import math
import jax, jax.numpy as jnp
from jax import lax
import numpy as np

D_MODEL = 2048
BATCH = 16
SEQ = 256
DEPTH = 1
DEC_BATCH = 2
DEC_SEQ = 4096
PAST_LEN = 256

GRID_W = 64
NORM_EPS = 1e-6
N_MOD = 6
GDN_HEADS = 8
GDN_DK = 128
GDN_DV = 128
GDN_CONV = 3
GDN_CHUNK = 64
MLA_HEADS = 8
MLA_Q_LORA = 512
MLA_KV_LORA = 512
MLA_NOPE = 128
MLA_ROPE = 64
MLA_DV = 128
MLA_QK = MLA_NOPE + MLA_ROPE
ROPE_THETA = 10000.0
Q_BLOCK = 128
D_FF = 5632
FFN_CONV = 3
GDN_QKV_W = GDN_HEADS * (2 * GDN_DK + GDN_DV)
GDN_OUT_W = GDN_HEADS * GDN_DV
MLA_OUT_W = MLA_HEADS * MLA_DV
MIX_W = GDN_OUT_W + MLA_OUT_W
IN_SPLITS = (GDN_QKV_W, GDN_OUT_W, 2 * GDN_HEADS, 2 * GDN_HEADS, MLA_Q_LORA, MLA_KV_LORA, MLA_ROPE)
IN_COLS = GDN_QKV_W + GDN_OUT_W + 4 * GDN_HEADS + MLA_Q_LORA + MLA_KV_LORA + MLA_ROPE

kernel_name = 'hybrid_gdn_mla_prefix_dit_step'


def rms_norm(x, g):
    xf = x.astype(jnp.float32)
    y = xf * lax.rsqrt(jnp.mean(xf * xf, axis=-1, keepdims=True) + NORM_EPS)
    return (y * g.astype(jnp.float32)).astype(x.dtype)


def l2_normalize(x):
    xf = x.astype(jnp.float32)
    return xf * lax.rsqrt(jnp.sum(xf * xf, axis=-1, keepdims=True) + NORM_EPS)


def split_cols(x, sizes):
    parts, start = [], 0
    for size in sizes:
        parts.append(x[..., start:start + size])
        start += size
    return parts


def depthwise_conv_centred(x, w):
    k = w.shape[0]
    pad = k // 2
    t = x.shape[1]
    xp = jnp.pad(x, ((0, 0), (pad, pad), (0, 0)))
    y = xp[:, 0:t] * w[0]
    for i in range(1, k):
        y = y + xp[:, i:i + t] * w[i]
    return y


def ada_modulation(cond, w_ada, b_ada):
    m = jax.nn.silu(cond) @ w_ada + b_ada
    return jnp.split(m[:, None, :], N_MOD, axis=-1)


def modulate(h, shift, scale):
    return h * (1 + scale) + shift


def axial_rope(t):
    rows = t // GRID_W
    row = jnp.repeat(jnp.arange(rows, dtype=jnp.float32), GRID_W)
    col = jnp.tile(jnp.arange(GRID_W, dtype=jnp.float32), rows)
    n_freq = MLA_ROPE // 4
    inv_freq = ROPE_THETA ** (-jnp.arange(n_freq, dtype=jnp.float32) / n_freq)
    ang = jnp.concatenate([row[:, None] * inv_freq, col[:, None] * inv_freq], axis=-1)
    return jnp.cos(ang), jnp.sin(ang)


def rotate_half(x, cos, sin):
    x1, x2 = jnp.split(x.astype(jnp.float32), 2, axis=-1)
    return jnp.concatenate([x1 * cos - x2 * sin, x1 * sin + x2 * cos], axis=-1).astype(x.dtype)


def gdn_chunked(q, k, v, g, beta, s0):
    bsz, t, h, dk = q.shape
    dv = v.shape[-1]
    c = GDN_CHUNK
    n = t // c

    def chunks(x):
        return x.reshape(bsz, n, c, h, x.shape[-1]).transpose(1, 0, 3, 2, 4)

    q = chunks(q) * (dk ** -0.5)
    k = chunks(k)
    v = chunks(v)
    g = jnp.cumsum(g.reshape(bsz, n, c, h).transpose(1, 0, 3, 2), axis=-1)
    beta = beta.reshape(bsz, n, c, h).transpose(1, 0, 3, 2)
    k_beta = k * beta[..., None]
    v_beta = v * beta[..., None]
    causal = jnp.tril(jnp.ones((c, c), dtype=bool))
    strict = jnp.tril(jnp.ones((c, c), dtype=bool), k=-1)
    decay = jnp.exp(jnp.where(causal, g[..., :, None] - g[..., None, :], -jnp.inf))
    a_mat = jnp.where(strict, jnp.einsum('nbhid,nbhjd->nbhij', k_beta, k) * decay, 0.0)
    eye = jnp.eye(c, dtype=jnp.float32)
    t_mat = lax.linalg.triangular_solve(eye + a_mat, jnp.broadcast_to(eye, a_mat.shape),
                                        left_side=True, lower=True, unit_diagonal=True)
    u = jnp.einsum('nbhij,nbhjd->nbhid', t_mat, v_beta)
    w = jnp.einsum('nbhij,nbhjd->nbhid', t_mat, k_beta * jnp.exp(g)[..., None])
    intra = jnp.where(causal, jnp.einsum('nbhid,nbhjd->nbhij', q, k) * decay, 0.0)

    def step(state, inp):
        q_c, k_c, u_c, w_c, g_c, intra_c = inp
        v_new = u_c - jnp.einsum('bhck,bhkv->bhcv', w_c, state)
        o_c = (jnp.einsum('bhck,bhkv->bhcv', q_c * jnp.exp(g_c)[..., None], state)
               + jnp.einsum('bhcj,bhjv->bhcv', intra_c, v_new))
        g_last = g_c[..., -1:]
        state = (state * jnp.exp(g_last)[..., None]
                 + jnp.einsum('bhck,bhcv->bhkv', k_c * jnp.exp(g_last - g_c)[..., None], v_new))
        return state, o_c

    s_fin, o = lax.scan(step, s0, (q, k, u, w, g, intra))
    return o.transpose(1, 0, 3, 2, 4).reshape(bsz, t, h, dv), s_fin


def gdn_mix(qkv, z, a, b, conv_w, a_log, dt_bias, norm_g, s0):
    bsz, t, _ = qkv.shape
    qkv = jax.nn.silu(depthwise_conv_centred(qkv, conv_w))
    q, k, v = split_cols(qkv, (GDN_HEADS * GDN_DK, GDN_HEADS * GDN_DK, GDN_HEADS * GDN_DV))
    q = l2_normalize(q.reshape(bsz, t, GDN_HEADS, GDN_DK))
    k = l2_normalize(k.reshape(bsz, t, GDN_HEADS, GDN_DK))
    v = v.reshape(bsz, t, GDN_HEADS, GDN_DV).astype(jnp.float32)
    a = a.astype(jnp.float32).reshape(bsz, t, 2, GDN_HEADS)
    g = -jnp.exp(a_log.astype(jnp.float32)) * jax.nn.softplus(a + dt_bias.astype(jnp.float32))
    beta = jax.nn.sigmoid(b.astype(jnp.float32).reshape(bsz, t, 2, GDN_HEADS))
    s0 = s0.astype(jnp.float32)
    o_f, s_f = gdn_chunked(q, k, v, g[:, :, 0], beta[:, :, 0], s0[:, 0])
    o_b, s_b = gdn_chunked(q[:, ::-1], k[:, ::-1], v[:, ::-1], g[:, ::-1, 1], beta[:, ::-1, 1], s0[:, 1])
    o = o_f + o_b[:, ::-1]
    o = rms_norm(o, norm_g) * jax.nn.silu(z.reshape(bsz, t, GDN_HEADS, GDN_DV).astype(jnp.float32))
    return o.reshape(bsz, t, GDN_OUT_W).astype(z.dtype), jnp.stack([s_f, s_b], axis=1)


def mla_expand(ckv, k_pe, w_kv_b):
    bsz, t, _ = ckv.shape
    kv = (ckv @ w_kv_b).reshape(bsz, t, MLA_HEADS, MLA_NOPE + MLA_DV)
    k_nope, v = kv[..., :MLA_NOPE], kv[..., MLA_NOPE:]
    k_rope = jnp.broadcast_to(k_pe[:, :, None, :], (bsz, t, MLA_HEADS, MLA_ROPE)).astype(k_nope.dtype)
    return jnp.concatenate([k_nope, k_rope], axis=-1), v


def softmax_attend(q, k, v):
    s = jnp.einsum('bqhd,bkhd->bhqk', q, k).astype(jnp.float32) * (MLA_QK ** -0.5)
    p = jax.nn.softmax(s, axis=-1).astype(v.dtype)
    return jnp.einsum('bhqk,bkhd->bqhd', p, v)


def blockwise_attend(q, k, v):
    bsz, t, h, d = q.shape
    nblk = t // Q_BLOCK
    qb = q.reshape(bsz, nblk, Q_BLOCK, h, d).transpose(1, 0, 2, 3, 4)
    ob = lax.map(lambda qi: softmax_attend(qi, k, v), qb)
    return ob.transpose(1, 0, 2, 3, 4).reshape(bsz, t, h, v.shape[-1])


def conv_ffn(h, w_up, conv_w, conv_b, w_down):
    u = depthwise_conv_centred(h @ w_up, conv_w) + conv_b
    a, gate = jnp.split(u, 2, axis=-1)
    return (jax.nn.silu(a) * gate) @ w_down


def trunk_layer(x, cond, p, ctx):
    bsz, t, _ = x.shape
    shift1, scale1, gate1, shift2, scale2, gate2 = ada_modulation(cond, p['w_ada'], p['b_ada'])
    h = modulate(rms_norm(x, p['norm1_g']), shift1, scale1)
    qkv, z, a, b, q_a, kv_a, k_pe = split_cols(h @ p['w_in'], IN_SPLITS)
    if ctx is None:
        s0 = jnp.zeros((bsz, 2, GDN_HEADS, GDN_DK, GDN_DV), jnp.float32)
    else:
        s0 = ctx[0]
    gdn_o, s_fin = gdn_mix(qkv, z, a, b, p['gdn_conv_w'], p['gdn_a_log'], p['gdn_dt_bias'],
                           p['gdn_norm_g'], s0)
    q = (rms_norm(q_a, p['mla_q_norm_g']) @ p['mla_w_q_b']).reshape(bsz, t, MLA_HEADS, MLA_QK)
    ckv = rms_norm(kv_a, p['mla_kv_norm_g'])
    if ctx is None:
        k, v = mla_expand(ckv, k_pe, p['mla_w_kv_b'])
        mla_o = softmax_attend(q, k, v)
    else:
        cos, sin = axial_rope(t)
        q = jnp.concatenate([q[..., :MLA_NOPE],
                             rotate_half(q[..., MLA_NOPE:], cos[None, :, None, :], sin[None, :, None, :])], axis=-1)
        k_pe = rotate_half(k_pe, cos[None], sin[None])
        k_lat, v_lat = mla_expand(ckv, k_pe, p['mla_w_kv_b'])
        k_ctx, v_ctx = mla_expand(ctx[1].astype(ckv.dtype), ctx[2], p['mla_w_kv_b'])
        k = jnp.concatenate([k_lat, k_ctx], axis=1)
        v = jnp.concatenate([v_lat, v_ctx], axis=1)
        mla_o = blockwise_attend(q, k, v)
    mix = jnp.concatenate([gdn_o, mla_o.reshape(bsz, t, MLA_OUT_W)], axis=-1) @ p['w_out']
    x = x + gate1 * mix
    h2 = modulate(rms_norm(x, p['norm2_g']), shift2, scale2)
    x = x + gate2 * conv_ffn(h2, p['w_up'], p['ffn_conv_w'], p['ffn_conv_b'], p['w_down'])
    if ctx is None:
        return x, s_fin, ckv, k_pe
    return x


def setup_inputs(seed: int = 0) -> dict:
    key = jax.random.key(seed)
    ks = jax.random.split(key, 28)
    f32 = jnp.float32
    L, D = DEPTH, D_MODEL

    def nrm(k, shape, scale):
        return jax.random.normal(k, shape, f32) * scale

    def gain(k, shape):
        return 1.0 + 0.02 * jax.random.normal(k, shape, f32)

    a_init = jax.random.uniform(ks[9], (L, 2, GDN_HEADS), f32, 1.0, 16.0)
    dt = jnp.exp(jax.random.uniform(ks[10], (L, 2, GDN_HEADS), f32, math.log(1e-3), math.log(1e-1)))
    return {
        'x_prompt': nrm(ks[0], (BATCH, SEQ, D), 1.0),
        'x_sample': nrm(ks[1], (DEC_BATCH, DEC_SEQ, D), 1.0),
        'state_gdn': nrm(ks[2], (DEC_BATCH, L, 2, GDN_HEADS, GDN_DK, GDN_DV), 0.1),
        'cache_mla_ckv': nrm(ks[3], (DEC_BATCH, L, PAST_LEN, MLA_KV_LORA), 1.0),
        'cache_mla_kpe': nrm(ks[4], (DEC_BATCH, L, PAST_LEN, MLA_ROPE), 1.0),
        'c': nrm(ks[5], (DEC_BATCH, D), 1.0),
        'c_ctx': nrm(ks[6], (D,), 1.0),
        'w_ada': nrm(ks[7], (L, D, N_MOD * D), 0.5 * D ** -0.5),
        'b_ada': nrm(ks[8], (L, N_MOD * D), 0.01),
        'norm1_g': gain(ks[11], (L, D)),
        'w_in': nrm(ks[12], (L, D, IN_COLS), D ** -0.5),
        'gdn_conv_w': nrm(ks[13], (L, GDN_CONV, GDN_QKV_W), GDN_CONV ** -0.5),
        'gdn_a_log': jnp.log(a_init),
        'gdn_dt_bias': dt + jnp.log(-jnp.expm1(-dt)),
        'gdn_norm_g': gain(ks[14], (L, GDN_DV)),
        'mla_q_norm_g': gain(ks[15], (L, MLA_Q_LORA)),
        'mla_w_q_b': nrm(ks[16], (L, MLA_Q_LORA, MLA_HEADS * MLA_QK), MLA_Q_LORA ** -0.5),
        'mla_kv_norm_g': gain(ks[17], (L, MLA_KV_LORA)),
        'mla_w_kv_b': nrm(ks[18], (L, MLA_KV_LORA, MLA_HEADS * (MLA_NOPE + MLA_DV)), MLA_KV_LORA ** -0.5),
        'w_out': nrm(ks[19], (L, MIX_W, D), MIX_W ** -0.5),
        'norm2_g': gain(ks[20], (L, D)),
        'w_up': nrm(ks[21], (L, D, 2 * D_FF), D ** -0.5),
        'ffn_conv_w': nrm(ks[22], (L, FFN_CONV, 2 * D_FF), FFN_CONV ** -0.5),
        'ffn_conv_b': nrm(ks[23], (L, 2 * D_FF), 0.01),
        'w_down': nrm(ks[24], (L, D_FF, D), D_FF ** -0.5),
        'final_norm_g': gain(ks[25], (D,)),
    }


def reference(x_prompt, x_sample, state_gdn, cache_mla_ckv, cache_mla_kpe, c, c_ctx,
              w_ada, b_ada, norm1_g, w_in, gdn_conv_w, gdn_a_log, gdn_dt_bias, gdn_norm_g,
              mla_q_norm_g, mla_w_q_b, mla_kv_norm_g, mla_w_kv_b, w_out, norm2_g,
              w_up, ffn_conv_w, ffn_conv_b, w_down, final_norm_g):
    cond_ctx = jnp.broadcast_to(c_ctx[None, :], (x_prompt.shape[0], c_ctx.shape[-1]))
    yp, ys = x_prompt, x_sample
    states, ckvs, kpes = [], [], []
    for l in range(DEPTH):
        p = {
            'w_ada': w_ada[l], 'b_ada': b_ada[l], 'norm1_g': norm1_g[l], 'w_in': w_in[l],
            'gdn_conv_w': gdn_conv_w[l], 'gdn_a_log': gdn_a_log[l], 'gdn_dt_bias': gdn_dt_bias[l],
            'gdn_norm_g': gdn_norm_g[l], 'mla_q_norm_g': mla_q_norm_g[l], 'mla_w_q_b': mla_w_q_b[l],
            'mla_kv_norm_g': mla_kv_norm_g[l], 'mla_w_kv_b': mla_w_kv_b[l], 'w_out': w_out[l],
            'norm2_g': norm2_g[l], 'w_up': w_up[l], 'ffn_conv_w': ffn_conv_w[l],
            'ffn_conv_b': ffn_conv_b[l], 'w_down': w_down[l],
        }
        yp, s_l, ckv_l, kpe_l = trunk_layer(yp, cond_ctx, p, None)
        states.append(s_l)
        ckvs.append(ckv_l)
        kpes.append(kpe_l)
        ys = trunk_layer(ys, c, p, (state_gdn[:, l], cache_mla_ckv[:, l], cache_mla_kpe[:, l]))
    y_prompt = rms_norm(yp, final_norm_g)
    y_sample = rms_norm(ys, final_norm_g)
    new_state_gdn = jnp.stack(states, axis=1)
    new_cache_mla_ckv = jnp.stack(ckvs, axis=1)
    new_cache_mla_kpe = jnp.stack(kpes, axis=1)
    return (y_prompt, y_sample, new_state_gdn, new_cache_mla_ckv, new_cache_mla_kpe)
```

```python
import functools

import jax
import jax.numpy as jnp
from jax import lax
from jax.experimental import pallas as pl
from jax.experimental.pallas import tpu as pltpu

F32 = jnp.float32
BF16 = jnp.bfloat16

D_MODEL = 2048
GRID_W = 64
NORM_EPS = 1e-6
N_MOD = 6
GDN_HEADS = 8
GDN_DK = 128
GDN_DV = 128
GDN_CHUNK = 64
MLA_HEADS = 8
MLA_Q_LORA = 512
MLA_KV_LORA = 512
MLA_NOPE = 128
MLA_ROPE = 64
MLA_DV = 128
MLA_QK = MLA_NOPE + MLA_ROPE
ROPE_THETA = 10000.0
D_FF = 5632
GDN_QKV_W = GDN_HEADS * (2 * GDN_DK + GDN_DV)
GDN_OUT_W = GDN_HEADS * GDN_DV
MLA_OUT_W = MLA_HEADS * MLA_DV

LANES = 128
MAIN_W = GDN_QKV_W + GDN_OUT_W + MLA_Q_LORA + MLA_KV_LORA
Z_COL0 = GDN_QKV_W
QA_COL0 = GDN_QKV_W + GDN_OUT_W
KVA_COL0 = QA_COL0 + MLA_Q_LORA
MISC_KPE = 0
MISC_A = MLA_ROPE
MISC_B = MLA_ROPE + 2 * GDN_HEADS
MLA_QPAD = 2 * LANES

VMEM_LIMIT = 48 * 1024 * 1024


def _cparams(sem):
    return pltpu.CompilerParams(dimension_semantics=sem, vmem_limit_bytes=VMEM_LIMIT)


def _silu(x):
    return x / (1.0 + jnp.exp(-x))


def _softplus(x):
    return jnp.maximum(x, 0.0) + jnp.log1p(jnp.exp(-jnp.abs(x)))


def _rms(x, g):
    return x * lax.rsqrt(jnp.mean(x * x, axis=-1, keepdims=True) + NORM_EPS) * g


def _mm(a, b):
    return jnp.dot(a.astype(BF16), b.astype(BF16), preferred_element_type=F32)


def _mm_nt(a, b):
    return lax.dot_general(a.astype(BF16), b.astype(BF16), (((1,), (1,)), ((), ())),
                           preferred_element_type=F32)


def _mm_tn(a, b):
    return lax.dot_general(a.astype(BF16), b.astype(BF16), (((0,), (0,)), ((), ())),
                           preferred_element_type=F32)


def _split3(a):
    a1 = a.astype(BF16)
    r = a - a1.astype(F32)
    a2 = r.astype(BF16)
    a3 = (r - a2.astype(F32)).astype(BF16)
    return a1, a2, a3


def _mm_x3(a, b):
    a1, a2, _ = _split3(a)
    b1, b2, _ = _split3(b)
    d = functools.partial(jnp.dot, preferred_element_type=F32)
    return d(a1, b1) + (d(a1, b2) + d(a2, b1))


def _mm_exact01(a, m01):
    a1, a2, a3 = _split3(a)
    m = m01.astype(BF16)
    d = functools.partial(jnp.dot, preferred_element_type=F32)
    return d(a1, m) + (d(a2, m) + d(a3, m))


def _ada_kernel(c_ref, w_ref, b_ref, o_ref):
    s = _silu(c_ref[...])
    o_ref[...] = _mm(s, w_ref[...]) + b_ref[...]


def _ada(cond8, w_ada, b_ada, tn=512):
    d, n = w_ada.shape
    return pl.pallas_call(
        _ada_kernel,
        out_shape=jax.ShapeDtypeStruct((8, n), F32),
        grid=(n // tn,),
        in_specs=[pl.BlockSpec((8, d), lambda j: (0, 0)),
                  pl.BlockSpec((d, tn), lambda j: (0, j)),
                  pl.BlockSpec((1, tn), lambda j: (0, j))],
        out_specs=pl.BlockSpec((8, tn), lambda j: (0, j)),
        compiler_params=_cparams(("arbitrary",)),
        name="ada_mod",
    )(cond8, w_ada, b_ada.reshape(1, n))


def _norm_mm_kernel(*refs, normalize, modulated, has_extra, emit_h):
    it = iter(refs)
    x_ref = next(it)
    g_ref = next(it) if normalize else None
    sh_ref = next(it) if modulated else None
    sc_ref = next(it) if modulated else None
    w_ref = next(it)
    wx_ref = next(it) if has_extra else None
    o_ref = next(it)
    ox_ref = next(it) if has_extra else None
    hn_ref = next(it) if emit_h else None
    h_scr = next(it)

    @pl.when(pl.program_id(1) == 0)
    def _():
        h = x_ref[...]
        if normalize:
            h = _rms(h, g_ref[...])
        if emit_h:
            hn_ref[...] = h
        if modulated:
            h = h * (1.0 + sc_ref[...]) + sh_ref[...]
        hb = h.astype(BF16)
        h_scr[...] = hb
        if has_extra:
            ox_ref[...] = jnp.dot(hb, wx_ref[...], preferred_element_type=F32)

    o_ref[...] = jnp.dot(h_scr[...], w_ref[...], preferred_element_type=F32).astype(o_ref.dtype)


def _norm_mm(x, w, *, xcol=0, k=None, g=None, shift=None, scale=None, w_extra=None,
             emit_h=False, out_dtype=F32, tm=512, tn=512, name="norm_mm"):
    m = x.shape[0]
    k = x.shape[1] if k is None else k
    n = w.shape[1]
    tn = min(tn, n)
    tm = min(tm, m)
    normalize = g is not None
    modulated = shift is not None
    has_extra = w_extra is not None
    ins = [x]
    in_specs = [pl.BlockSpec((tm, k), lambda i, j: (i, xcol))]
    if normalize:
        ins.append(g.reshape(1, k))
        in_specs.append(pl.BlockSpec((1, k), lambda i, j: (0, 0)))
    if modulated:
        tiles_per_group = (m // shift.shape[0]) // tm
        for a in (shift, scale):
            ins.append(a)
            in_specs.append(pl.BlockSpec((None, 1, k), lambda i, j: (i // tiles_per_group, 0, 0)))
    ins.append(w)
    in_specs.append(pl.BlockSpec((k, tn), lambda i, j: (0, j)))
    out_shape = [jax.ShapeDtypeStruct((m, n), out_dtype)]
    out_specs = [pl.BlockSpec((tm, tn), lambda i, j: (i, j))]
    if has_extra:
        nx = w_extra.shape[1]
        ins.append(w_extra)
        in_specs.append(pl.BlockSpec((k, nx), lambda i, j: (0, 0)))
        out_shape.append(jax.ShapeDtypeStruct((m, nx), F32))
        out_specs.append(pl.BlockSpec((tm, nx), lambda i, j: (i, 0)))
    if emit_h:
        out_shape.append(jax.ShapeDtypeStruct((m, k), F32))
        out_specs.append(pl.BlockSpec((tm, k), lambda i, j: (i, 0)))
    return pl.pallas_call(
        functools.partial(_norm_mm_kernel, normalize=normalize, modulated=modulated,
                          has_extra=has_extra, emit_h=emit_h),
        out_shape=out_shape,
        grid=(m // tm, n // tn),
        in_specs=in_specs,
        out_specs=out_specs,
        scratch_shapes=[pltpu.VMEM((tm, k), BF16)],
        compiler_params=_cparams(("parallel", "arbitrary")),
        name=name,
    )(*ins)


def _gdn_kernel(*refs, t, has_s0):
    (q_ref, k_ref, v_ref, z_ref, misc_ref, cwq_ref, cwk_ref, cwv_ref, ar0_ref, ar1_ref,
     alog_ref, dtb_ref, ng_ref) = refs[:13]
    rest = refs[13:]
    if has_s0:
        s0_ref, rest = rest[0], rest[1:]
    o_ref, sfin_ref, qs, ks, vs, of_s, ob_s, grow_s = rest
    c = GDN_CHUNK
    n_chunks = t // c
    head = pl.program_id(1)

    row = lax.broadcasted_iota(jnp.int32, (t, 1), 0)

    def conv_silu(x_ref, cw_ref):
        x = x_ref[...]
        cw = cw_ref[...]
        prev = jnp.where(row == 0, 0.0, pltpu.roll(x, 1, 0))
        nxt = jnp.where(row == t - 1, 0.0, pltpu.roll(x, t - 1, 0))
        return _silu(prev * cw[0:1] + x * cw[1:2] + nxt * cw[2:3])

    def l2n(x):
        return x * lax.rsqrt(jnp.sum(x * x, axis=-1, keepdims=True) + NORM_EPS)

    qs[...] = l2n(conv_silu(q_ref, cwq_ref)) * (GDN_DK ** -0.5)
    ks[...] = l2n(conv_silu(k_ref, cwk_ref))
    vs[...] = conv_silu(v_ref, cwv_ref)

    coef = -jnp.exp(alog_ref[...])
    dtb = dtb_ref[...]

    kk_i = lax.broadcasted_iota(jnp.int32, (c, c), 0)
    jj_i = lax.broadcasted_iota(jnp.int32, (c, c), 1)
    for d, ar_ref in ((0, ar0_ref), (1, ar1_ref)):
        g_rows = coef[d:d + 1, :c] * _softplus(ar_ref[...] + dtb[d:d + 1, :c])
        ones01 = jnp.where((kk_i <= jj_i) if d == 0 else (kk_i >= jj_i), 1.0, 0.0)
        grow_s[d] = _mm_exact01(g_rows, ones01)

    lane = lax.broadcasted_iota(jnp.int32, (c, LANES), 1)
    rowi = lax.broadcasted_iota(jnp.int32, (c, LANES), 0)
    eye = jnp.where(kk_i == jj_i, 1.0, 0.0)

    def chunk_step(ci, d, state):
        r0 = pl.multiple_of(ci * c, c)
        qc = qs[pl.ds(r0, c), :]
        kc = ks[pl.ds(r0, c), :]
        vc = vs[pl.ds(r0, c), :]
        mc = misc_ref[pl.ds(r0, c), :]
        a_col = jnp.sum(jnp.where(lane == MISC_A + d * GDN_HEADS + head, mc, 0.0), axis=1, keepdims=True)
        b_col = jnp.sum(jnp.where(lane == MISC_B + d * GDN_HEADS + head, mc, 0.0), axis=1, keepdims=True)
        g = coef[d:d + 1] * _softplus(a_col + dtb[d:d + 1])
        beta = 1.0 / (1.0 + jnp.exp(-b_col))
        gc = g
        for s in (1, 2, 4, 8, 16, 32):
            if d == 0:
                gc = gc + jnp.where(rowi >= s, pltpu.roll(gc, s, 0), 0.0)
            else:
                gc = gc + jnp.where(rowi < c - s, pltpu.roll(gc, c - s, 0), 0.0)
        g_row = grow_s[d, pl.ds(ci, 1), :]
        if d == 0:
            causal, strict = kk_i >= jj_i, kk_i > jj_i
        else:
            causal, strict = kk_i <= jj_i, kk_i < jj_i
        decay = jnp.exp(jnp.where(causal, gc[:, :c] - g_row, -jnp.inf))
        kb = kc * beta
        a_mat = jnp.where(strict, _mm_nt(kb, kc) * decay, 0.0)
        intra = jnp.where(causal, _mm_nt(qc, kc) * decay, 0.0)
        neg = -a_mat
        t_mat = eye + neg
        pw = _mm_x3(neg, neg)
        for lvl in range(5):
            t_mat = t_mat + _mm_x3(t_mat, pw)
            if lvl < 4:
                pw = _mm_x3(pw, pw)
        eg = jnp.exp(gc)
        u = _mm(t_mat, vc * beta)
        w = _mm(t_mat, kb * eg)
        v_new = u - _mm(w, state)
        o = _mm(qc * eg, state) + _mm(intra, v_new)
        g_last = gc[c - 1:c, :] if d == 0 else gc[0:1, :]
        new_state = state * jnp.exp(g_last) + _mm_tn(kc * jnp.exp(g_last - gc), v_new)
        return o, new_state

    if has_s0:
        init = (s0_ref[0], s0_ref[1])
    else:
        init = (jnp.zeros((GDN_DK, GDN_DV), F32), jnp.zeros((GDN_DK, GDN_DV), F32))

    def body(n, carry):
        s_f, s_b = carry
        o_f, s_f = chunk_step(n, 0, s_f)
        of_s[pl.ds(pl.multiple_of(n * c, c), c), :] = o_f
        nb = n_chunks - 1 - n
        o_b, s_b = chunk_step(nb, 1, s_b)
        ob_s[pl.ds(pl.multiple_of(nb * c, c), c), :] = o_b
        return s_f, s_b

    s_f, s_b = lax.fori_loop(0, n_chunks, body, init)
    sfin_ref[0] = s_f
    sfin_ref[1] = s_b
    o = _rms(of_s[...] + ob_s[...], ng_ref[...]) * _silu(z_ref[...])
    o_ref[...] = o.astype(o_ref.dtype)


def _gdn(proj, misc, a_rows, conv_w, alog_b, dtb_b, norm_g, s0, bsz, t):
    h = GDN_HEADS
    n_chunks = a_rows.shape[2]
    has_s0 = s0 is not None

    def col(off):
        return pl.BlockSpec((None, t, LANES), lambda b, hh: (b, 0, off + hh))

    def cw(off):
        return pl.BlockSpec((3, LANES), lambda b, hh: (0, off + hh))

    in_specs = [col(0), col(h), col(2 * h), col(Z_COL0 // LANES),
                pl.BlockSpec((None, t, LANES), lambda b, hh: (b, 0, 0)),
                cw(0), cw(h), cw(2 * h),
                pl.BlockSpec((None, None, n_chunks, GDN_CHUNK), lambda b, hh: (b, hh, 0, 0)),
                pl.BlockSpec((None, None, n_chunks, GDN_CHUNK), lambda b, hh: (b, h + hh, 0, 0)),
                pl.BlockSpec((None, 2, LANES), lambda b, hh: (hh, 0, 0)),
                pl.BlockSpec((None, 2, LANES), lambda b, hh: (hh, 0, 0)),
                pl.BlockSpec((1, LANES), lambda b, hh: (0, 0))]
    ins = [proj, proj, proj, proj, misc, conv_w, conv_w, conv_w, a_rows, a_rows, alog_b, dtb_b,
           norm_g.reshape(1, GDN_DV)]
    if has_s0:
        in_specs.append(pl.BlockSpec((None, 2, None, GDN_DK, GDN_DV), lambda b, hh: (b, 0, hh, 0, 0)))
        ins.append(s0)
    return pl.pallas_call(
        functools.partial(_gdn_kernel, t=t, has_s0=has_s0),
        out_shape=[jax.ShapeDtypeStruct((bsz, t, GDN_OUT_W), BF16),
                   jax.ShapeDtypeStruct((bsz, 2, h, GDN_DK, GDN_DV), F32)],
        grid=(bsz, h),
        in_specs=in_specs,
        out_specs=[pl.BlockSpec((None, t, LANES), lambda b, hh: (b, 0, hh)),
                   pl.BlockSpec((None, 2, None, GDN_DK, GDN_DV), lambda b, hh: (b, 0, hh, 0, 0))],
        scratch_shapes=[pltpu.VMEM((t, LANES), F32)] * 5
        + [pltpu.VMEM((2, n_chunks, GDN_CHUNK), F32)],
        compiler_params=_cparams(("parallel", "parallel")),
        name="gdn",
    )(*ins)


def _rope(x, cos, sin):
    half = MLA_ROPE // 2
    x1, x2 = x[:, :half], x[:, half:]
    return jnp.concatenate([x1 * cos - x2 * sin, x1 * sin + x2 * cos], axis=1)


def _attn_kernel(*refs, rope, two_sets, s1, s2):
    it = iter(refs)
    q_ref, kv1_ref, m1_ref = next(it), next(it), next(it)
    kv2_ref = next(it) if two_sets else None
    kp2_ref = next(it) if two_sets else None
    if rope:
        cq_ref, sq_ref, ck_ref, sk_ref = next(it), next(it), next(it), next(it)
    o_ref, kfull = next(it), next(it)

    @pl.when(pl.program_id(2) == 0)
    def _():
        kr = m1_ref[:, MISC_KPE:MISC_KPE + MLA_ROPE]
        if rope:
            kr = _rope(kr, ck_ref[...], sk_ref[...])
        kfull[0:s1, 0:LANES] = kv1_ref[:, 0:MLA_NOPE]
        kfull[0:s1, LANES:2 * LANES] = jnp.concatenate(
            [kr, jnp.zeros((s1, LANES - MLA_ROPE), F32)], axis=1).astype(BF16)
        if two_sets:
            kfull[s1:s1 + s2, 0:LANES] = kv2_ref[:, 0:MLA_NOPE]
            kfull[s1:s1 + s2, LANES:2 * LANES] = jnp.concatenate(
                [kp2_ref[...], jnp.zeros((s2, LANES - MLA_ROPE), F32)], axis=1).astype(BF16)

    if rope:
        q = q_ref[...]
        tq = q.shape[0]
        qr = _rope(q[:, MLA_NOPE:MLA_QK], cq_ref[...], sq_ref[...])
        qf = jnp.concatenate([q[:, :MLA_NOPE], qr, jnp.zeros((tq, LANES - MLA_ROPE), F32)],
                             axis=1).astype(BF16)
    else:
        qf = q_ref[...]
    s = _mm_nt(qf, kfull[...]) * (MLA_QK ** -0.5)
    m = jnp.max(s, axis=-1, keepdims=True)
    p = jnp.exp(s - m)
    l = jnp.sum(p, axis=-1, keepdims=True)
    pb = p.astype(BF16)
    o = jnp.dot(pb[:, :s1], kv1_ref[:, MLA_NOPE:], preferred_element_type=F32)
    if two_sets:
        o = o + jnp.dot(pb[:, s1:], kv2_ref[:, MLA_NOPE:], preferred_element_type=F32)
    o_ref[...] = (o / l).astype(o_ref.dtype)


def _attn(q, kv1, misc1, kv2=None, kpe2=None, rope_tabs=None, tq=256):
    bsz, t, _ = q.shape
    s1 = kv1.shape[1]
    two_sets = kv2 is not None
    s2 = kv2.shape[1] if two_sets else 0
    rope = rope_tabs is not None
    tq = min(tq, t)
    hw = MLA_QPAD
    ins = [q, kv1, misc1]
    in_specs = [pl.BlockSpec((None, tq, hw), lambda b, h, i: (b, i, h)),
                pl.BlockSpec((None, s1, hw), lambda b, h, i: (b, 0, h)),
                pl.BlockSpec((None, s1, LANES), lambda b, h, i: (b, 0, 0))]
    if two_sets:
        ins += [kv2, kpe2]
        in_specs += [pl.BlockSpec((None, s2, hw), lambda b, h, i: (b, 0, h)),
                     pl.BlockSpec((None, s2, MLA_ROPE), lambda b, h, i: (b, 0, 0))]
    if rope:
        cos, sin = rope_tabs
        half = MLA_ROPE // 2
        ins += [cos, sin, cos, sin]
        in_specs += [pl.BlockSpec((tq, half), lambda b, h, i: (i, 0)),
                     pl.BlockSpec((tq, half), lambda b, h, i: (i, 0)),
                     pl.BlockSpec((s1, half), lambda b, h, i: (0, 0)),
                     pl.BlockSpec((s1, half), lambda b, h, i: (0, 0))]
    return pl.pallas_call(
        functools.partial(_attn_kernel, rope=rope, two_sets=two_sets, s1=s1, s2=s2),
        out_shape=jax.ShapeDtypeStruct((bsz, t, MLA_OUT_W), BF16),
        grid=(bsz, MLA_HEADS, t // tq),
        in_specs=in_specs,
        out_specs=pl.BlockSpec((None, tq, MLA_DV), lambda b, h, i: (b, i, h)),
        scratch_shapes=[pltpu.VMEM((s1 + s2, hw), BF16)],
        compiler_params=_cparams(("parallel", "parallel", "arbitrary")),
        name="mla_attn",
    )(*ins)


def _outproj_kernel(a1_ref, a2_ref, w1_ref, w2_ref, x_ref, gate_ref, o_ref):
    acc = jnp.dot(a1_ref[...], w1_ref[...], preferred_element_type=F32)
    acc = acc + jnp.dot(a2_ref[...], w2_ref[...], preferred_element_type=F32)
    o_ref[...] = x_ref[...] + gate_ref[...] * acc


def _outproj(a1, a2, w, x, gate, tm=512, tn=512):
    m, k1 = a1.shape
    k2 = a2.shape[1]
    n = w.shape[1]
    tiles_per_group = (m // gate.shape[0]) // tm
    return pl.pallas_call(
        _outproj_kernel,
        out_shape=jax.ShapeDtypeStruct((m, n), F32),
        grid=(m // tm, n // tn),
        in_specs=[pl.BlockSpec((tm, k1), lambda i, j: (i, 0)),
                  pl.BlockSpec((tm, k2), lambda i, j: (i, 0)),
                  pl.BlockSpec((k1, tn), lambda i, j: (0, j)),
                  pl.BlockSpec((k2, tn), lambda i, j: (k1 // k2, j)),
                  pl.BlockSpec((tm, tn), lambda i, j: (i, j)),
                  pl.BlockSpec((None, 1, tn), lambda i, j: (i // tiles_per_group, 0, j))],
        out_specs=pl.BlockSpec((tm, tn), lambda i, j: (i, j)),
        compiler_params=_cparams(("parallel", "arbitrary")),
        name="out_proj",
    )(a1, a2, w, w, x, gate)


def _ffn_kernel(x_ref, xp_ref, xn_ref, g_ref, sh_ref, sc_ref, gate_ref, wa_ref, wg_ref,
                cwa_ref, cwg_ref, cba_ref, cbg_ref, wd_ref, fg_ref, o_ref,
                h_scr, hh_scr, acc_scr, *, t, tm):
    i = pl.program_id(0)
    j = pl.program_id(1)

    def normmod(x):
        return _rms(x, g_ref[...]) * (1.0 + sc_ref[...]) + sh_ref[...]

    @pl.when(j == 0)
    def _():
        h_scr[...] = normmod(x_ref[...]).astype(BF16)
        hh_scr[...] = jnp.concatenate([normmod(xp_ref[...]), normmod(xn_ref[...])], axis=0).astype(BF16)
        acc_scr[...] = jnp.zeros_like(acc_scr)

    h = h_scr[...]
    hh = hh_scr[...]
    row = lax.broadcasted_iota(jnp.int32, (tm, 1), 0)
    pos = (i * tm + row) % t
    first_row, last_row = row == 0, row == tm - 1
    no_prev, no_next = pos == 0, pos == t - 1

    def conv_half(w_ref, cw_ref, cb_ref):
        up = jnp.dot(h, w_ref[...], preferred_element_type=F32)
        halo = jnp.dot(hh, w_ref[...], preferred_element_type=F32)
        prev = jnp.where(first_row, halo[7:8], pltpu.roll(up, 1, 0))
        prev = jnp.where(no_prev, 0.0, prev)
        nxt = jnp.where(last_row, halo[8:9], pltpu.roll(up, tm - 1, 0))
        nxt = jnp.where(no_next, 0.0, nxt)
        cw = cw_ref[...]
        return prev * cw[0:1] + up * cw[1:2] + nxt * cw[2:3] + cb_ref[...]

    a = conv_half(wa_ref, cwa_ref, cba_ref)
    gt = conv_half(wg_ref, cwg_ref, cbg_ref)
    act = (_silu(a) * gt).astype(BF16)
    acc_scr[...] += jnp.dot(act, wd_ref[...], preferred_element_type=F32)

    @pl.when(j == pl.num_programs(1) - 1)
    def _():
        x2 = x_ref[...] + gate_ref[...] * acc_scr[...]
        o_ref[...] = _rms(x2, fg_ref[...])


def _ffn(x, norm_g, shift, scale, gate, w_up, conv_w, conv_b, w_down, final_g, t, tm=512, tn=512):
    m, d = x.shape
    ff = w_down.shape[0]
    nj = ff // tn
    tiles_per_group = (m // shift.shape[0]) // tm
    r8 = tm // 8
    last8 = m // 8 - 1

    def grp(i, j):
        return (i // tiles_per_group, 0, 0)

    row_vec = lambda n: pl.BlockSpec((1, n), lambda i, j: (0, 0))
    return pl.pallas_call(
        functools.partial(_ffn_kernel, t=t, tm=tm),
        out_shape=jax.ShapeDtypeStruct((m, d), F32),
        grid=(m // tm, nj),
        in_specs=[pl.BlockSpec((tm, d), lambda i, j: (i, 0)),
                  pl.BlockSpec((8, d), lambda i, j: (jnp.maximum(i * r8 - 1, 0), 0)),
                  pl.BlockSpec((8, d), lambda i, j: (jnp.minimum((i + 1) * r8, last8), 0)),
                  row_vec(d),
                  pl.BlockSpec((None, 1, d), grp),
                  pl.BlockSpec((None, 1, d), grp),
                  pl.BlockSpec((None, 1, d), grp),
                  pl.BlockSpec((d, tn), lambda i, j: (0, j)),
                  pl.BlockSpec((d, tn), lambda i, j: (0, nj + j)),
                  pl.BlockSpec((3, tn), lambda i, j: (0, j)),
                  pl.BlockSpec((3, tn), lambda i, j: (0, nj + j)),
                  pl.BlockSpec((1, tn), lambda i, j: (0, j)),
                  pl.BlockSpec((1, tn), lambda i, j: (0, nj + j)),
                  pl.BlockSpec((tn, d), lambda i, j: (j, 0)),
                  row_vec(d)],
        out_specs=pl.BlockSpec((tm, d), lambda i, j: (i, 0)),
        scratch_shapes=[pltpu.VMEM((tm, d), BF16), pltpu.VMEM((16, d), BF16), pltpu.VMEM((tm, d), F32)],
        compiler_params=_cparams(("parallel", "arbitrary")),
        name="conv_ffn",
    )(x, x, x, norm_g.reshape(1, d), shift, scale, gate, w_up, w_up, conv_w, conv_w,
      conv_b.reshape(1, -1), conv_b.reshape(1, -1), w_down, final_g.reshape(1, d))


def _axial_rope_tables(t):
    rows = t // GRID_W
    row = jnp.repeat(jnp.arange(rows, dtype=F32), GRID_W)
    col = jnp.tile(jnp.arange(GRID_W, dtype=F32), rows)
    n_freq = MLA_ROPE // 4
    inv_freq = ROPE_THETA ** (-jnp.arange(n_freq, dtype=F32) / n_freq)
    ang = jnp.concatenate([row[:, None] * inv_freq, col[:, None] * inv_freq], axis=-1)
    return jnp.cos(ang), jnp.sin(ang)


def _trunk(x, mod, p, ctx):
    bsz, t, d = x.shape
    m = bsz * t
    x2d = x.reshape(m, d)
    mods = [mod[:, i].reshape(-1, 1, d) for i in range(N_MOD)]
    shift1, scale1, gate1, shift2, scale2, gate2 = mods

    proj, misc = _norm_mm(x2d, p['w_in_main'], g=p['norm1_g'], shift=shift1, scale=scale1,
                          w_extra=p['w_in_misc'], name="in_proj")
    a_rows = misc[:, MISC_A:MISC_A + 2 * GDN_HEADS].reshape(bsz, t // GDN_CHUNK, GDN_CHUNK, 2 * GDN_HEADS)
    a_rows = a_rows.transpose(0, 3, 1, 2)
    a_rows = jnp.pad(a_rows, ((0, 0), (0, 0), (0, max(0, 8 - t // GDN_CHUNK)), (0, 0)))
    s0 = None if ctx is None else ctx[0]
    gdn_o, s_fin = _gdn(proj.reshape(bsz, t, MAIN_W), misc.reshape(bsz, t, LANES), a_rows,
                        p['gdn_conv_w'], p['alog_b'], p['dtb_b'], p['gdn_norm_g'], s0, bsz, t)

    q = _norm_mm(proj, p['w_q_b'], xcol=QA_COL0 // MLA_Q_LORA, k=MLA_Q_LORA, g=p['mla_q_norm_g'],
                 out_dtype=BF16 if ctx is None else F32, tn=MLA_HEADS * MLA_QPAD, name="q_proj")[0]
    kv, ckv = _norm_mm(proj, p['w_kv_b'], xcol=KVA_COL0 // MLA_KV_LORA, k=MLA_KV_LORA,
                       g=p['mla_kv_norm_g'], emit_h=True, out_dtype=BF16,
                       tn=MLA_HEADS * MLA_QPAD, name="kv_proj")
    hq = MLA_HEADS * MLA_QPAD
    if ctx is None:
        mla_o = _attn(q.reshape(bsz, t, hq), kv.reshape(bsz, t, hq), misc.reshape(bsz, t, LANES))
    else:
        past = ctx[1].shape[1]
        kv_c = _norm_mm(ctx[1].reshape(bsz * past, MLA_KV_LORA), p['w_kv_b'], out_dtype=BF16,
                        tn=hq, name="kv_proj_cache")[0]
        mla_o = _attn(q.reshape(bsz, t, hq), kv.reshape(bsz, t, hq), misc.reshape(bsz, t, LANES),
                      kv_c.reshape(bsz, past, hq), ctx[2], rope_tabs=_axial_rope_tables(t))

    x1 = _outproj(gdn_o.reshape(m, GDN_OUT_W), mla_o.reshape(m, MLA_OUT_W), p['w_out'], x2d, gate1)
    y = _ffn(x1, p['norm2_g'], shift2, scale2, gate2, p['w_up'], p['ffn_conv_w'], p['ffn_conv_b'],
             p['w_down'], p['final_norm_g'], t)
    y = y.reshape(bsz, t, d)
    if ctx is None:
        kpe = misc[:, MISC_KPE:MISC_KPE + MLA_ROPE].reshape(bsz, t, MLA_ROPE)
        return y, s_fin, ckv.reshape(bsz, t, MLA_KV_LORA), kpe
    return y


def kernel(x_prompt, x_sample, state_gdn, cache_mla_ckv, cache_mla_kpe, c, c_ctx, w_ada, b_ada, norm1_g, w_in, gdn_conv_w, gdn_a_log, gdn_dt_bias, gdn_norm_g, mla_q_norm_g, mla_w_q_b, mla_kv_norm_g, mla_w_kv_b, w_out, norm2_g, w_up, ffn_conv_w, ffn_conv_b, w_down, final_norm_g):
    depth = w_in.shape[0]
    assert depth == 1, "single-layer trunk"
    d = D_MODEL
    dec_b = x_sample.shape[0]
    assert 1 + dec_b <= 8

    cond8 = jnp.zeros((8, d), F32).at[0].set(c_ctx).at[1:1 + dec_b].set(c)
    mod = _ada(cond8, w_ada[0], b_ada[0]).reshape(8, N_MOD, d)

    wi = w_in[0]
    a0 = GDN_QKV_W + GDN_OUT_W
    qa0 = a0 + 4 * GDN_HEADS
    kpe0 = qa0 + MLA_Q_LORA + MLA_KV_LORA
    w_in_main = jnp.concatenate([wi[:, :a0], wi[:, qa0:kpe0]], axis=1).astype(BF16)
    w_in_misc = jnp.concatenate([wi[:, kpe0:], wi[:, a0:qa0],
                                 jnp.zeros((d, LANES - MLA_ROPE - 4 * GDN_HEADS), F32)], axis=1).astype(BF16)
    w_q_b = jnp.pad(mla_w_q_b[0].reshape(MLA_Q_LORA, MLA_HEADS, MLA_QK),
                    ((0, 0), (0, 0), (0, MLA_QPAD - MLA_QK))).reshape(MLA_Q_LORA, -1).astype(BF16)

    def lane_bcast(v):
        return jnp.broadcast_to(v.T[:, :, None], (GDN_HEADS, 2, LANES)).astype(F32)

    p = {
        'w_in_main': w_in_main, 'w_in_misc': w_in_misc, 'norm1_g': norm1_g[0],
        'gdn_conv_w': gdn_conv_w[0], 'alog_b': lane_bcast(gdn_a_log[0]), 'dtb_b': lane_bcast(gdn_dt_bias[0]),
        'gdn_norm_g': gdn_norm_g[0], 'mla_q_norm_g': mla_q_norm_g[0], 'w_q_b': w_q_b,
        'mla_kv_norm_g': mla_kv_norm_g[0], 'w_kv_b': mla_w_kv_b[0].astype(BF16),
        'w_out': w_out[0].astype(BF16), 'norm2_g': norm2_g[0], 'w_up': w_up[0].astype(BF16),
        'ffn_conv_w': ffn_conv_w[0], 'ffn_conv_b': ffn_conv_b[0], 'w_down': w_down[0].astype(BF16),
        'final_norm_g': final_norm_g,
    }

    y_prompt, s_fin, ckv, kpe = _trunk(x_prompt, mod[0:1], p, None)
    y_sample = _trunk(x_sample, mod[1:1 + dec_b], p,
                      (state_gdn[:, 0], cache_mla_ckv[:, 0], cache_mla_kpe[:, 0]))
    return (y_prompt, y_sample, s_fin[:, None], ckv[:, None], kpe[:, None])
```

```python
import functools

import jax
import jax.numpy as jnp
from jax import lax
from jax.experimental import pallas as pl
from jax.experimental.pallas import tpu as pltpu

F32 = jnp.float32
BF16 = jnp.bfloat16

D_MODEL = 2048
GRID_W = 64
NORM_EPS = 1e-6
N_MOD = 6
GDN_HEADS = 8
GDN_DK = 128
GDN_DV = 128
MLA_HEADS = 8
MLA_Q_LORA = 512
MLA_KV_LORA = 512
MLA_NOPE = 128
MLA_ROPE = 64
MLA_DV = 128
MLA_QK = MLA_NOPE + MLA_ROPE
ROPE_THETA = 10000.0
D_FF = 5632
GDN_QKV_W = GDN_HEADS * (2 * GDN_DK + GDN_DV)
GDN_OUT_W = GDN_HEADS * GDN_DV
MLA_OUT_W = MLA_HEADS * MLA_DV

LANES = 128
GDN_CHUNK = LANES
MAIN_W = GDN_QKV_W + GDN_OUT_W + MLA_Q_LORA + MLA_KV_LORA
Z_COL0 = GDN_QKV_W
QA_COL0 = GDN_QKV_W + GDN_OUT_W
KVA_COL0 = QA_COL0 + MLA_Q_LORA
MISC_KPE = 0
MISC_A = MLA_ROPE
MISC_B = MLA_ROPE + 2 * GDN_HEADS
MLA_QPAD = 2 * LANES

VMEM_LIMIT = 48 * 1024 * 1024


def _cparams(sem):
    return pltpu.CompilerParams(dimension_semantics=sem, vmem_limit_bytes=VMEM_LIMIT)


def _silu(x):
    return x / (1.0 + jnp.exp(-x))


def _softplus(x):
    return jnp.maximum(x, 0.0) + jnp.log1p(jnp.exp(-jnp.abs(x)))


def _rms(x, g):
    return x * lax.rsqrt(jnp.mean(x * x, axis=-1, keepdims=True) + NORM_EPS) * g


def _l2n(x):
    return x * lax.rsqrt(jnp.sum(x * x, axis=-1, keepdims=True) + NORM_EPS)


def _mm(a, b):
    return jnp.dot(a.astype(BF16), b.astype(BF16), preferred_element_type=F32)


def _mm_nt(a, b):
    return lax.dot_general(a.astype(BF16), b.astype(BF16), (((1,), (1,)), ((), ())),
                           preferred_element_type=F32)


def _mm_tn(a, b):
    return lax.dot_general(a.astype(BF16), b.astype(BF16), (((0,), (0,)), ((), ())),
                           preferred_element_type=F32)


def _split2(a):
    a1 = a.astype(BF16)
    return a1, (a - a1.astype(F32)).astype(BF16)


def _mm_x3(a, b):
    a1, a2 = _split2(a)
    b1, b2 = _split2(b)
    d = functools.partial(jnp.dot, preferred_element_type=F32)
    hi = d(jnp.concatenate([a1, a2], axis=1), jnp.concatenate([b1, b1], axis=0))
    return hi + d(a1, b2)


def _seq_conv3(up, halo, cw, row, pos, tm, t):
    prev = jnp.where(row == 0, halo[7:8], pltpu.roll(up, 1, 0))
    prev = jnp.where(pos == 0, 0.0, prev)
    nxt = jnp.where(row == tm - 1, halo[8:9], pltpu.roll(up, tm - 1, 0))
    nxt = jnp.where(pos == t - 1, 0.0, nxt)
    return prev * cw[0:1] + up * cw[1:2] + nxt * cw[2:3]


def _ada_kernel(c_ref, w_ref, b_ref, o_ref):
    s = _silu(c_ref[...])
    o_ref[...] = _mm(s, w_ref[...]) + b_ref[...]


def _ada(cond8, w_ada, b_ada, tn=512):
    d, n = w_ada.shape
    return pl.pallas_call(
        _ada_kernel,
        out_shape=jax.ShapeDtypeStruct((8, n), F32),
        grid=(n // tn,),
        in_specs=[pl.BlockSpec((8, d), lambda j: (0, 0)),
                  pl.BlockSpec((d, tn), lambda j: (0, j)),
                  pl.BlockSpec((1, tn), lambda j: (0, j))],
        out_specs=pl.BlockSpec((8, tn), lambda j: (0, j)),
        compiler_params=_cparams(("arbitrary",)),
        name="ada_mod",
    )(cond8, w_ada, b_ada.reshape(1, n))


def _in_proj_kernel(x_ref, xp_ref, xn_ref, g_ref, sh_ref, sc_ref, w_ref, wx_ref, cw_ref,
                    o_ref, ox_ref, h_scr, hh_scr, *, t, tm, tn):
    i = pl.program_id(0)
    j = pl.program_id(1)
    n_qk = 2 * GDN_HEADS * GDN_DK // tn
    n_q = GDN_HEADS * GDN_DK // tn
    n_conv = GDN_QKV_W // tn

    def normmod(x):
        return _rms(x, g_ref[...]) * (1.0 + sc_ref[...]) + sh_ref[...]

    @pl.when(j == 0)
    def _():
        hb = normmod(x_ref[...]).astype(BF16)
        h_scr[...] = hb
        hh_scr[...] = jnp.concatenate([normmod(xp_ref[...]), normmod(xn_ref[...])], axis=0).astype(BF16)
        ox_ref[...] = jnp.dot(hb, wx_ref[...], preferred_element_type=F32)

    up = jnp.dot(h_scr[...], w_ref[...], preferred_element_type=F32)

    @pl.when(j < n_conv)
    def _():
        halo = jnp.dot(hh_scr[...], w_ref[...], preferred_element_type=F32)
        row = lax.broadcasted_iota(jnp.int32, (tm, 1), 0)
        pos = (i * tm + row) % t
        y = _silu(_seq_conv3(up, halo, cw_ref[...], row, pos, tm, t))

        @pl.when(j < n_qk)
        def _():
            qscale = jnp.where(j < n_q, GDN_DK ** -0.5, 1.0)
            for hd in range(tn // GDN_DK):
                sl = slice(hd * GDN_DK, (hd + 1) * GDN_DK)
                o_ref[:, sl] = _l2n(y[:, sl]) * qscale

        @pl.when(j >= n_qk)
        def _():
            o_ref[...] = y

    @pl.when(j >= n_conv)
    def _():
        o_ref[...] = up


def _in_proj(x, norm_g, shift, scale, w_main, w_misc, conv_w, t, tm=512, tn=512):
    m, d = x.shape
    n = w_main.shape[1]
    nx = w_misc.shape[1]
    tiles_per_group = (m // shift.shape[0]) // tm
    r8 = tm // 8
    last8 = m // 8 - 1
    last_conv = GDN_QKV_W // tn - 1

    def grp(i, j):
        return (i // tiles_per_group, 0, 0)

    return pl.pallas_call(
        functools.partial(_in_proj_kernel, t=t, tm=tm, tn=tn),
        out_shape=[jax.ShapeDtypeStruct((m, n), F32), jax.ShapeDtypeStruct((m, nx), F32)],
        grid=(m // tm, n // tn),
        in_specs=[pl.BlockSpec((tm, d), lambda i, j: (i, 0)),
                  pl.BlockSpec((8, d), lambda i, j: (jnp.maximum(i * r8 - 1, 0), 0)),
                  pl.BlockSpec((8, d), lambda i, j: (jnp.minimum((i + 1) * r8, last8), 0)),
                  pl.BlockSpec((1, d), lambda i, j: (0, 0)),
                  pl.BlockSpec((None, 1, d), grp),
                  pl.BlockSpec((None, 1, d), grp),
                  pl.BlockSpec((d, tn), lambda i, j: (0, j)),
                  pl.BlockSpec((d, nx), lambda i, j: (0, 0)),
                  pl.BlockSpec((3, tn), lambda i, j: (0, jnp.minimum(j, last_conv)))],
        out_specs=[pl.BlockSpec((tm, tn), lambda i, j: (i, j)),
                   pl.BlockSpec((tm, nx), lambda i, j: (i, 0))],
        scratch_shapes=[pltpu.VMEM((tm, d), BF16), pltpu.VMEM((16, d), BF16)],
        compiler_params=_cparams(("parallel", "arbitrary")),
        name="in_proj",
    )(x, x, x, norm_g.reshape(1, d), shift, scale, w_main, w_misc, conv_w)


def _norm_mm_kernel(*refs, normalize, emit_h):
    it = iter(refs)
    x_ref = next(it)
    g_ref = next(it) if normalize else None
    w_ref = next(it)
    o_ref = next(it)
    hn_ref = next(it) if emit_h else None
    h_scr = next(it)

    @pl.when(pl.program_id(1) == 0)
    def _():
        h = x_ref[...]
        if normalize:
            h = _rms(h, g_ref[...])
        if emit_h:
            hn_ref[...] = h
        h_scr[...] = h.astype(BF16)

    o_ref[...] = jnp.dot(h_scr[...], w_ref[...], preferred_element_type=F32).astype(o_ref.dtype)


def _norm_mm(x, w, *, xcol=0, k=None, g=None, emit_h=False, out_dtype=F32, tm=512, tn=512,
             name="norm_mm"):
    m = x.shape[0]
    k = x.shape[1] if k is None else k
    n = w.shape[1]
    tn = min(tn, n)
    tm = min(tm, m)
    normalize = g is not None
    ins = [x]
    in_specs = [pl.BlockSpec((tm, k), lambda i, j: (i, xcol))]
    if normalize:
        ins.append(g.reshape(1, k))
        in_specs.append(pl.BlockSpec((1, k), lambda i, j: (0, 0)))
    ins.append(w)
    in_specs.append(pl.BlockSpec((k, tn), lambda i, j: (0, j)))
    out_shape = [jax.ShapeDtypeStruct((m, n), out_dtype)]
    out_specs = [pl.BlockSpec((tm, tn), lambda i, j: (i, j))]
    if emit_h:
        out_shape.append(jax.ShapeDtypeStruct((m, k), F32))
        out_specs.append(pl.BlockSpec((tm, k), lambda i, j: (i, 0)))
    return pl.pallas_call(
        functools.partial(_norm_mm_kernel, normalize=normalize, emit_h=emit_h),
        out_shape=out_shape,
        grid=(m // tm, n // tn),
        in_specs=in_specs,
        out_specs=out_specs,
        scratch_shapes=[pltpu.VMEM((tm, k), BF16)],
        compiler_params=_cparams(("parallel", "arbitrary")),
        name=name,
    )(*ins)


def _gdn_kernel(*refs, cb, nb, hb, has_s0):
    (qf_ref, kf_ref, vf_ref, mf_ref, qb_ref, kb_ref, vb_ref, mb_ref, alog_ref, dtb_ref) = refs[:10]
    rest = refs[10:]
    if has_s0:
        s0_ref, rest = rest[0], rest[1:]
    of_ref, ob_ref, sfin_ref, st_scr = rest
    c = GDN_CHUNK
    tb = pl.program_id(2)
    head0 = pl.program_id(1) * hb

    @pl.when(tb == 0)
    def _():
        if has_s0:
            for hi in range(hb):
                st_scr[hi] = s0_ref[:, hi]
        else:
            st_scr[...] = jnp.zeros_like(st_scr)

    ii = lax.broadcasted_iota(jnp.int32, (c, c), 0)
    jj = lax.broadcasted_iota(jnp.int32, (c, c), 1)
    eye = jnp.where(ii == jj, 1.0, 0.0)
    masks = ((ii >= jj, ii > jj), (ii <= jj, ii < jj))

    chains = [(hi, d, ci) for hi in range(hb) for d in (0, 1) for ci in range(cb)]
    srcs = ((qf_ref, kf_ref, vf_ref, mf_ref), (qb_ref, kb_ref, vb_ref, mb_ref))

    def load(ch):
        hi, d, ci = ch
        q_ref, k_ref, v_ref, m_ref = srcs[d]
        rows = slice(ci * c, (ci + 1) * c)
        cols = slice(hi * c, (hi + 1) * c)
        mc = m_ref[rows, :]
        head = head0 + hi
        a_col = jnp.sum(jnp.where(jj == MISC_A + d * GDN_HEADS + head, mc, 0.0), axis=1, keepdims=True)
        b_col = jnp.sum(jnp.where(jj == MISC_B + d * GDN_HEADS + head, mc, 0.0), axis=1, keepdims=True)
        coef = -jnp.exp(alog_ref[hi, d:d + 1, :])
        g = coef * _softplus(a_col + dtb_ref[hi, d:d + 1, :])
        beta = 1.0 / (1.0 + jnp.exp(-b_col))
        for s in (1, 2, 4, 8, 16, 32, 64):
            if d == 0:
                g = g + jnp.where(ii >= s, pltpu.roll(g, s, 0), 0.0)
            else:
                g = g + jnp.where(ii < c - s, pltpu.roll(g, c - s, 0), 0.0)
        return q_ref[rows, cols], k_ref[rows, cols], v_ref[rows, cols], g, beta

    data = [load(ch) for ch in chains]
    decay = [jnp.exp(jnp.where(masks[ch[1]][0], g - g.T, -jnp.inf))
             for ch, (_, _, _, g, _) in zip(chains, data)]
    kbeta = [k * beta for (_, k, _, _, beta) in data]
    a_mat = [jnp.where(masks[ch[1]][1], _mm_nt(kb, k) * dc, 0.0)
             for ch, kb, (_, k, _, _, _), dc in zip(chains, kbeta, data, decay)]
    intra = [jnp.where(masks[ch[1]][0], _mm_nt(q, k) * dc, 0.0).astype(BF16)
             for ch, (q, k, _, _, _), dc in zip(chains, data, decay)]
    def in_blocks(s, x):
        k = s.bit_length() - 1
        return ((ii >> k) == (jj >> k)) if x else ((ii >> k) != (jj >> k))

    t_mat = [eye - jnp.where(in_blocks(2, True) & in_blocks(1, False), a, 0.0) for a in a_mat]
    s = 2
    while s < c:
        off = in_blocks(2 * s, True) & in_blocks(s, False)
        y = [_mm_x3(jnp.where(off, a, 0.0), x) for a, x in zip(a_mat, t_mat)]
        t_mat = [x - _mm_x3(x, y_) for x, y_ in zip(t_mat, y)]
        s *= 2
    eg = [jnp.exp(g) for (_, _, _, g, _) in data]
    u = [_mm(tm_, v * beta) for tm_, (_, _, v, _, beta) in zip(t_mat, data)]
    w = [_mm(tm_, kb * e) for tm_, kb, e in zip(t_mat, kbeta, eg)]
    w_qg = [jnp.concatenate([w_.astype(BF16), (q * e).astype(BF16)], axis=0)
            for w_, (q, _, _, _, _), e in zip(w, data, eg)]
    g_last = [(g[c - 1:c, :] if ch[1] == 0 else g[0:1, :]) for ch, (_, _, _, g, _) in zip(chains, data)]
    kg = [(k * jnp.exp(gl - g)).astype(BF16) for (_, k, _, g, _), gl in zip(data, g_last)]
    e_last = [jnp.exp(gl) for gl in g_last]

    idx = {ch: n for n, ch in enumerate(chains)}
    for hi in range(hb):
        for d, out_ref in ((0, of_ref), (1, ob_ref)):
            state = st_scr[hi, d]
            for step in range(cb):
                ci = step if d == 0 else cb - 1 - step
                n = idx[(hi, d, ci)]
                ws = jnp.dot(w_qg[n], state.astype(BF16), preferred_element_type=F32)
                v_new = u[n] - ws[:c]
                o = ws[c:] + jnp.dot(intra[n], v_new.astype(BF16), preferred_element_type=F32)
                out_ref[ci * c:(ci + 1) * c, hi * c:(hi + 1) * c] = o
                state = state * e_last[n] + _mm_tn(kg[n], v_new)
            st_scr[hi, d] = state

    @pl.when(tb == nb - 1)
    def _():
        for hi in range(hb):
            sfin_ref[:, hi] = st_scr[hi]


def _gdn(proj, misc, alog_b, dtb_b, s0, bsz, t, cb, hb):
    h = GDN_HEADS
    tblk = cb * GDN_CHUNK
    nb = t // tblk
    hw = hb * LANES
    has_s0 = s0 is not None

    def fwd(off):
        return pl.BlockSpec((None, tblk, hw), lambda b, hg, tb: (b, tb, off + hg))

    def bwd(off):
        return pl.BlockSpec((None, tblk, hw), lambda b, hg, tb: (b, nb - 1 - tb, off + hg))

    hg_n = h // hb
    in_specs = [fwd(0), fwd(hg_n), fwd(2 * hg_n),
                pl.BlockSpec((None, tblk, LANES), lambda b, hg, tb: (b, tb, 0)),
                bwd(0), bwd(hg_n), bwd(2 * hg_n),
                pl.BlockSpec((None, tblk, LANES), lambda b, hg, tb: (b, nb - 1 - tb, 0)),
                pl.BlockSpec((hb, 2, LANES), lambda b, hg, tb: (hg, 0, 0)),
                pl.BlockSpec((hb, 2, LANES), lambda b, hg, tb: (hg, 0, 0))]
    ins = [proj, proj, proj, misc, proj, proj, proj, misc, alog_b, dtb_b]
    st_spec = pl.BlockSpec((None, 2, hb, GDN_DK, GDN_DV), lambda b, hg, tb: (b, 0, hg, 0, 0))
    if has_s0:
        in_specs.append(st_spec)
        ins.append(s0)
    return pl.pallas_call(
        functools.partial(_gdn_kernel, cb=cb, nb=nb, hb=hb, has_s0=has_s0),
        out_shape=[jax.ShapeDtypeStruct((bsz, t, GDN_OUT_W), F32),
                   jax.ShapeDtypeStruct((bsz, t, GDN_OUT_W), F32),
                   jax.ShapeDtypeStruct((bsz, 2, h, GDN_DK, GDN_DV), F32)],
        grid=(bsz, hg_n, nb),
        in_specs=in_specs,
        out_specs=[pl.BlockSpec((None, tblk, hw), lambda b, hg, tb: (b, tb, hg)),
                   pl.BlockSpec((None, tblk, hw), lambda b, hg, tb: (b, nb - 1 - tb, hg)),
                   st_spec],
        scratch_shapes=[pltpu.VMEM((hb, 2, GDN_DK, GDN_DV), F32)],
        compiler_params=_cparams(("parallel", "parallel", "arbitrary")),
        name="gdn",
    )(*ins)


def _rope(x, cos, sin):
    half = MLA_ROPE // 2
    x1, x2 = x[:, :half], x[:, half:]
    return jnp.concatenate([x1 * cos - x2 * sin, x1 * sin + x2 * cos], axis=1)


def _attn_kernel(*refs, rope, two_sets, s1, s2):
    it = iter(refs)
    q_ref, kv1_ref, m1_ref = next(it), next(it), next(it)
    kv2_ref = next(it) if two_sets else None
    kp2_ref = next(it) if two_sets else None
    if rope:
        cq_ref, sq_ref, ck_ref, sk_ref = next(it), next(it), next(it), next(it)
    o_ref, kfull = next(it), next(it)

    @pl.when(pl.program_id(2) == 0)
    def _():
        kr = m1_ref[:, MISC_KPE:MISC_KPE + MLA_ROPE]
        if rope:
            kr = _rope(kr, ck_ref[...], sk_ref[...])
        kfull[0:s1, 0:LANES] = kv1_ref[:, 0:MLA_NOPE]
        kfull[0:s1, LANES:2 * LANES] = jnp.concatenate(
            [kr, jnp.zeros((s1, LANES - MLA_ROPE), F32)], axis=1).astype(BF16)
        if two_sets:
            kfull[s1:s1 + s2, 0:LANES] = kv2_ref[:, 0:MLA_NOPE]
            kfull[s1:s1 + s2, LANES:2 * LANES] = jnp.concatenate(
                [kp2_ref[...], jnp.zeros((s2, LANES - MLA_ROPE), F32)], axis=1).astype(BF16)

    if rope:
        q = q_ref[...]
        tq = q.shape[0]
        qr = _rope(q[:, MLA_NOPE:MLA_QK], cq_ref[...], sq_ref[...])
        qf = jnp.concatenate([q[:, :MLA_NOPE], qr, jnp.zeros((tq, LANES - MLA_ROPE), F32)],
                             axis=1).astype(BF16)
    else:
        qf = q_ref[...]
    s = _mm_nt(qf, kfull[...]) * (MLA_QK ** -0.5)
    m = jnp.max(s, axis=-1, keepdims=True)
    p = jnp.exp(s - m)
    l = jnp.sum(p, axis=-1, keepdims=True)
    pb = p.astype(BF16)
    o = jnp.dot(pb[:, :s1], kv1_ref[:, MLA_NOPE:], preferred_element_type=F32)
    if two_sets:
        o = o + jnp.dot(pb[:, s1:], kv2_ref[:, MLA_NOPE:], preferred_element_type=F32)
    o_ref[...] = (o / l).astype(o_ref.dtype)


def _attn(q, kv1, misc1, kv2=None, kpe2=None, rope_tabs=None, tq=256):
    bsz, t, _ = q.shape
    s1 = kv1.shape[1]
    two_sets = kv2 is not None
    s2 = kv2.shape[1] if two_sets else 0
    rope = rope_tabs is not None
    tq = min(tq, t)
    hw = MLA_QPAD
    ins = [q, kv1, misc1]
    in_specs = [pl.BlockSpec((None, tq, hw), lambda b, h, i: (b, i, h)),
                pl.BlockSpec((None, s1, hw), lambda b, h, i: (b, 0, h)),
                pl.BlockSpec((None, s1, LANES), lambda b, h, i: (b, 0, 0))]
    if two_sets:
        ins += [kv2, kpe2]
        in_specs += [pl.BlockSpec((None, s2, hw), lambda b, h, i: (b, 0, h)),
                     pl.BlockSpec((None, s2, MLA_ROPE), lambda b, h, i: (b, 0, 0))]
    if rope:
        cos, sin = rope_tabs
        half = MLA_ROPE // 2
        ins += [cos, sin, cos, sin]
        in_specs += [pl.BlockSpec((tq, half), lambda b, h, i: (i, 0)),
                     pl.BlockSpec((tq, half), lambda b, h, i: (i, 0)),
                     pl.BlockSpec((s1, half), lambda b, h, i: (0, 0)),
                     pl.BlockSpec((s1, half), lambda b, h, i: (0, 0))]
    return pl.pallas_call(
        functools.partial(_attn_kernel, rope=rope, two_sets=two_sets, s1=s1, s2=s2),
        out_shape=jax.ShapeDtypeStruct((bsz, t, MLA_OUT_W), BF16),
        grid=(bsz, MLA_HEADS, t // tq),
        in_specs=in_specs,
        out_specs=pl.BlockSpec((None, tq, MLA_DV), lambda b, h, i: (b, i, h)),
        scratch_shapes=[pltpu.VMEM((s1 + s2, hw), BF16)],
        compiler_params=_cparams(("parallel", "parallel", "arbitrary")),
        name="mla_attn",
    )(*ins)


def _outproj_kernel(of_ref, ob_ref, z_ref, ng_ref, a2_ref, w1_ref, w2_ref, x_ref, gate_ref,
                    o_ref, a1_scr):
    @pl.when(pl.program_id(1) == 0)
    def _():
        for hd in range(GDN_HEADS):
            sl = slice(hd * GDN_DV, (hd + 1) * GDN_DV)
            o = _rms(of_ref[:, sl] + ob_ref[:, sl], ng_ref[...]) * _silu(z_ref[:, sl])
            a1_scr[:, sl] = o.astype(BF16)

    acc = jnp.dot(a1_scr[...], w1_ref[...], preferred_element_type=F32)
    acc = acc + jnp.dot(a2_ref[...], w2_ref[...], preferred_element_type=F32)
    o_ref[...] = x_ref[...] + gate_ref[...] * acc


def _outproj(o_f, o_b, proj, norm_g, a2, w, x, gate, tm=512, tn=512):
    m, k1 = o_f.shape
    k2 = a2.shape[1]
    n = w.shape[1]
    tiles_per_group = (m // gate.shape[0]) // tm
    return pl.pallas_call(
        _outproj_kernel,
        out_shape=jax.ShapeDtypeStruct((m, n), F32),
        grid=(m // tm, n // tn),
        in_specs=[pl.BlockSpec((tm, k1), lambda i, j: (i, 0)),
                  pl.BlockSpec((tm, k1), lambda i, j: (i, 0)),
                  pl.BlockSpec((tm, k1), lambda i, j: (i, Z_COL0 // k1)),
                  pl.BlockSpec((1, GDN_DV), lambda i, j: (0, 0)),
                  pl.BlockSpec((tm, k2), lambda i, j: (i, 0)),
                  pl.BlockSpec((k1, tn), lambda i, j: (0, j)),
                  pl.BlockSpec((k2, tn), lambda i, j: (k1 // k2, j)),
                  pl.BlockSpec((tm, tn), lambda i, j: (i, j)),
                  pl.BlockSpec((None, 1, tn), lambda i, j: (i // tiles_per_group, 0, j))],
        out_specs=pl.BlockSpec((tm, tn), lambda i, j: (i, j)),
        scratch_shapes=[pltpu.VMEM((tm, k1), BF16)],
        compiler_params=_cparams(("parallel", "arbitrary")),
        name="out_proj",
    )(o_f, o_b, proj, norm_g.reshape(1, GDN_DV), a2, w, w, x, gate)


def _ffn_kernel(x_ref, xp_ref, xn_ref, g_ref, sh_ref, sc_ref, gate_ref, wa_ref, wg_ref,
                cwa_ref, cwg_ref, cba_ref, cbg_ref, wd_ref, fg_ref, o_ref,
                h_scr, hh_scr, acc_scr, *, t, tm):
    i = pl.program_id(0)
    j = pl.program_id(1)

    def normmod(x):
        return _rms(x, g_ref[...]) * (1.0 + sc_ref[...]) + sh_ref[...]

    @pl.when(j == 0)
    def _():
        h_scr[...] = normmod(x_ref[...]).astype(BF16)
        hh_scr[...] = jnp.concatenate([normmod(xp_ref[...]), normmod(xn_ref[...])], axis=0).astype(BF16)
        acc_scr[...] = jnp.zeros_like(acc_scr)

    h = h_scr[...]
    hh = hh_scr[...]
    row = lax.broadcasted_iota(jnp.int32, (tm, 1), 0)
    pos = (i * tm + row) % t

    def conv_half(w_ref, cw_ref, cb_ref):
        up = jnp.dot(h, w_ref[...], preferred_element_type=F32)
        halo = jnp.dot(hh, w_ref[...], preferred_element_type=F32)
        return _seq_conv3(up, halo, cw_ref[...], row, pos, tm, t) + cb_ref[...]

    a = conv_half(wa_ref, cwa_ref, cba_ref)
    gt = conv_half(wg_ref, cwg_ref, cbg_ref)
    act = (_silu(a) * gt).astype(BF16)
    acc_scr[...] += jnp.dot(act, wd_ref[...], preferred_element_type=F32)

    @pl.when(j == pl.num_programs(1) - 1)
    def _():
        x2 = x_ref[...] + gate_ref[...] * acc_scr[...]
        o_ref[...] = _rms(x2, fg_ref[...])


def _ffn(x, norm_g, shift, scale, gate, w_up, conv_w, conv_b, w_down, final_g, t, tm=512, tn=512):
    m, d = x.shape
    ff = w_down.shape[0]
    nj = ff // tn
    tiles_per_group = (m // shift.shape[0]) // tm
    r8 = tm // 8
    last8 = m // 8 - 1

    def grp(i, j):
        return (i // tiles_per_group, 0, 0)

    row_vec = lambda n: pl.BlockSpec((1, n), lambda i, j: (0, 0))
    return pl.pallas_call(
        functools.partial(_ffn_kernel, t=t, tm=tm),
        out_shape=jax.ShapeDtypeStruct((m, d), F32),
        grid=(m // tm, nj),
        in_specs=[pl.BlockSpec((tm, d), lambda i, j: (i, 0)),
                  pl.BlockSpec((8, d), lambda i, j: (jnp.maximum(i * r8 - 1, 0), 0)),
                  pl.BlockSpec((8, d), lambda i, j: (jnp.minimum((i + 1) * r8, last8), 0)),
                  row_vec(d),
                  pl.BlockSpec((None, 1, d), grp),
                  pl.BlockSpec((None, 1, d), grp),
                  pl.BlockSpec((None, 1, d), grp),
                  pl.BlockSpec((d, tn), lambda i, j: (0, j)),
                  pl.BlockSpec((d, tn), lambda i, j: (0, nj + j)),
                  pl.BlockSpec((3, tn), lambda i, j: (0, j)),
                  pl.BlockSpec((3, tn), lambda i, j: (0, nj + j)),
                  pl.BlockSpec((1, tn), lambda i, j: (0, j)),
                  pl.BlockSpec((1, tn), lambda i, j: (0, nj + j)),
                  pl.BlockSpec((tn, d), lambda i, j: (j, 0)),
                  row_vec(d)],
        out_specs=pl.BlockSpec((tm, d), lambda i, j: (i, 0)),
        scratch_shapes=[pltpu.VMEM((tm, d), BF16), pltpu.VMEM((16, d), BF16), pltpu.VMEM((tm, d), F32)],
        compiler_params=_cparams(("parallel", "arbitrary")),
        name="conv_ffn",
    )(x, x, x, norm_g.reshape(1, d), shift, scale, gate, w_up, w_up, conv_w, conv_w,
      conv_b.reshape(1, -1), conv_b.reshape(1, -1), w_down, final_g.reshape(1, d))


def _axial_rope_tables(t):
    rows = t // GRID_W
    row = jnp.repeat(jnp.arange(rows, dtype=F32), GRID_W)
    col = jnp.tile(jnp.arange(GRID_W, dtype=F32), rows)
    n_freq = MLA_ROPE // 4
    inv_freq = ROPE_THETA ** (-jnp.arange(n_freq, dtype=F32) / n_freq)
    ang = jnp.concatenate([row[:, None] * inv_freq, col[:, None] * inv_freq], axis=-1)
    return jnp.cos(ang), jnp.sin(ang)


def _gdn_blocking(t):
    cb = min(4, t // GDN_CHUNK)
    return cb, max(1, 4 // cb)


def _trunk(x, mod, p, ctx):
    bsz, t, d = x.shape
    m = bsz * t
    x2d = x.reshape(m, d)
    mods = [mod[:, i].reshape(-1, 1, d) for i in range(N_MOD)]
    shift1, scale1, gate1, shift2, scale2, gate2 = mods

    proj, misc = _in_proj(x2d, p['norm1_g'], shift1, scale1, p['w_in_main'], p['w_in_misc'],
                          p['gdn_conv_w'], t)
    s0 = None if ctx is None else ctx[0]
    cb, hb = _gdn_blocking(t)
    o_f, o_b, s_fin = _gdn(proj.reshape(bsz, t, MAIN_W), misc.reshape(bsz, t, LANES),
                           p['alog_b'], p['dtb_b'], s0, bsz, t, cb, hb)

    hq = MLA_HEADS * MLA_QPAD
    q = _norm_mm(proj, p['w_q_b'], xcol=QA_COL0 // MLA_Q_LORA, k=MLA_Q_LORA, g=p['mla_q_norm_g'],
                 out_dtype=BF16 if ctx is None else F32, tn=hq, name="q_proj")[0]
    kv, ckv = _norm_mm(proj, p['w_kv_b'], xcol=KVA_COL0 // MLA_KV_LORA, k=MLA_KV_LORA,
                       g=p['mla_kv_norm_g'], emit_h=True, out_dtype=BF16, tn=hq, name="kv_proj")
    if ctx is None:
        mla_o = _attn(q.reshape(bsz, t, hq), kv.reshape(bsz, t, hq), misc.reshape(bsz, t, LANES))
    else:
        past = ctx[1].shape[1]
        kv_c = _norm_mm(ctx[1].reshape(bsz * past, MLA_KV_LORA), p['w_kv_b'], out_dtype=BF16,
                        tn=hq, name="kv_proj_cache")[0]
        mla_o = _attn(q.reshape(bsz, t, hq), kv.reshape(bsz, t, hq), misc.reshape(bsz, t, LANES),
                      kv_c.reshape(bsz, past, hq), ctx[2], rope_tabs=_axial_rope_tables(t))

    x1 = _outproj(o_f.reshape(m, GDN_OUT_W), o_b.reshape(m, GDN_OUT_W), proj, p['gdn_norm_g'],
                  mla_o.reshape(m, MLA_OUT_W), p['w_out'], x2d, gate1)
    y = _ffn(x1, p['norm2_g'], shift2, scale2, gate2, p['w_up'], p['ffn_conv_w'], p['ffn_conv_b'],
             p['w_down'], p['final_norm_g'], t)
    y = y.reshape(bsz, t, d)
    if ctx is None:
        kpe = misc[:, MISC_KPE:MISC_KPE + MLA_ROPE].reshape(bsz, t, MLA_ROPE)
        return y, s_fin, ckv.reshape(bsz, t, MLA_KV_LORA), kpe
    return y


def _prep_params(w_in, gdn_conv_w, gdn_a_log, gdn_dt_bias, gdn_norm_g, mla_q_norm_g, mla_w_q_b,
                 mla_kv_norm_g, mla_w_kv_b, w_out, norm1_g, norm2_g, w_up, ffn_conv_w, ffn_conv_b,
                 w_down, final_norm_g):
    d = D_MODEL
    wi = w_in[0]
    a0 = GDN_QKV_W + GDN_OUT_W
    qa0 = a0 + 4 * GDN_HEADS
    kpe0 = qa0 + MLA_Q_LORA + MLA_KV_LORA
    w_in_main = jnp.concatenate([wi[:, :a0], wi[:, qa0:kpe0]], axis=1).astype(BF16)
    w_in_misc = jnp.concatenate([wi[:, kpe0:], wi[:, a0:qa0],
                                 jnp.zeros((d, LANES - MLA_ROPE - 4 * GDN_HEADS), F32)], axis=1).astype(BF16)
    w_q_b = jnp.pad(mla_w_q_b[0].reshape(MLA_Q_LORA, MLA_HEADS, MLA_QK),
                    ((0, 0), (0, 0), (0, MLA_QPAD - MLA_QK))).reshape(MLA_Q_LORA, -1).astype(BF16)

    def lane_bcast(v):
        return jnp.broadcast_to(v.T[:, :, None], (GDN_HEADS, 2, LANES)).astype(F32)

    return {
        'w_in_main': w_in_main, 'w_in_misc': w_in_misc, 'norm1_g': norm1_g[0],
        'gdn_conv_w': gdn_conv_w[0], 'alog_b': lane_bcast(gdn_a_log[0]), 'dtb_b': lane_bcast(gdn_dt_bias[0]),
        'gdn_norm_g': gdn_norm_g[0], 'mla_q_norm_g': mla_q_norm_g[0], 'w_q_b': w_q_b,
        'mla_kv_norm_g': mla_kv_norm_g[0], 'w_kv_b': mla_w_kv_b[0].astype(BF16),
        'w_out': w_out[0].astype(BF16), 'norm2_g': norm2_g[0], 'w_up': w_up[0].astype(BF16),
        'ffn_conv_w': ffn_conv_w[0], 'ffn_conv_b': ffn_conv_b[0], 'w_down': w_down[0].astype(BF16),
        'final_norm_g': final_norm_g,
    }


def kernel(x_prompt, x_sample, state_gdn, cache_mla_ckv, cache_mla_kpe, c, c_ctx, w_ada, b_ada, norm1_g, w_in, gdn_conv_w, gdn_a_log, gdn_dt_bias, gdn_norm_g, mla_q_norm_g, mla_w_q_b, mla_kv_norm_g, mla_w_kv_b, w_out, norm2_g, w_up, ffn_conv_w, ffn_conv_b, w_down, final_norm_g):
    assert w_in.shape[0] == 1, "single-layer trunk"
    d = D_MODEL
    dec_b = x_sample.shape[0]
    assert 1 + dec_b <= 8

    cond8 = jnp.zeros((8, d), F32).at[0].set(c_ctx).at[1:1 + dec_b].set(c)
    mod = _ada(cond8, w_ada[0], b_ada[0]).reshape(8, N_MOD, d)
    p = _prep_params(w_in, gdn_conv_w, gdn_a_log, gdn_dt_bias, gdn_norm_g, mla_q_norm_g, mla_w_q_b,
                     mla_kv_norm_g, mla_w_kv_b, w_out, norm1_g, norm2_g, w_up, ffn_conv_w, ffn_conv_b,
                     w_down, final_norm_g)

    y_prompt, s_fin, ckv, kpe = _trunk(x_prompt, mod[0:1], p, None)
    y_sample = _trunk(x_sample, mod[1:1 + dec_b], p,
                      (state_gdn[:, 0], cache_mla_ckv[:, 0], cache_mla_kpe[:, 0]))
    return (y_prompt, y_sample, s_fin[:, None], ckv[:, None], kpe[:, None])
```

```python
import functools

import jax
import jax.numpy as jnp
from jax import lax
from jax.experimental import pallas as pl
from jax.experimental.pallas import tpu as pltpu

F32 = jnp.float32
BF16 = jnp.bfloat16

D_MODEL = 2048
GRID_W = 64
NORM_EPS = 1e-6
N_MOD = 6
GDN_HEADS = 8
GDN_DK = 128
GDN_DV = 128
MLA_HEADS = 8
MLA_Q_LORA = 512
MLA_KV_LORA = 512
MLA_NOPE = 128
MLA_ROPE = 64
MLA_DV = 128
MLA_QK = MLA_NOPE + MLA_ROPE
ROPE_THETA = 10000.0
D_FF = 5632
GDN_QKV_W = GDN_HEADS * (2 * GDN_DK + GDN_DV)
GDN_OUT_W = GDN_HEADS * GDN_DV
MLA_OUT_W = MLA_HEADS * MLA_DV

LANES = 128
MXU_N = 256
GDN_CHUNK = LANES
MAIN_W = GDN_QKV_W + GDN_OUT_W + MLA_Q_LORA + MLA_KV_LORA
Z_COL0 = GDN_QKV_W
QA_COL0 = GDN_QKV_W + GDN_OUT_W
KVA_COL0 = QA_COL0 + MLA_Q_LORA
MISC_KPE = 0
MISC_A = MLA_ROPE
MISC_B = MLA_ROPE + 2 * GDN_HEADS
MLA_QPAD = 2 * LANES

VMEM_LIMIT = 48 * 1024 * 1024


def _cparams(sem):
    return pltpu.CompilerParams(dimension_semantics=sem, vmem_limit_bytes=VMEM_LIMIT)


def _silu(x):
    return x / (1.0 + jnp.exp(-x))


def _softplus(x):
    return jnp.maximum(x, 0.0) + jnp.log1p(jnp.exp(-jnp.abs(x)))


def _row_reduce(x, op, lane_reduce):
    acc = x[:, :LANES]
    for c0 in range(LANES, x.shape[-1], LANES):
        acc = op(acc, x[:, c0:c0 + LANES])
    return lane_reduce(acc, axis=-1, keepdims=True)


def _row_sum(x):
    return _row_reduce(x, jnp.add, jnp.sum)


def _row_max(x):
    return _row_reduce(x, jnp.maximum, jnp.max)


def _rms(x, g):
    return x * lax.rsqrt(_row_sum(x * x) * (1.0 / x.shape[-1]) + NORM_EPS) * g


def _l2n(x):
    return x * lax.rsqrt(jnp.sum(x * x, axis=-1, keepdims=True) + NORM_EPS)


def _mm(a, b):
    return jnp.dot(a.astype(BF16), b.astype(BF16), preferred_element_type=F32)


def _mm_nt(a, b):
    return lax.dot_general(a.astype(BF16), b.astype(BF16), (((1,), (1,)), ((), ())),
                           preferred_element_type=F32)


def _mm_tn(a, b):
    return lax.dot_general(a.astype(BF16), b.astype(BF16), (((0,), (0,)), ((), ())),
                           preferred_element_type=F32)


def _split2(a):
    a1 = a.astype(BF16)
    return a1, (a - a1.astype(F32)).astype(BF16)


def _mm_x3(a, b):
    a1, a2 = _split2(a)
    b1, b2 = _split2(b)
    d = functools.partial(jnp.dot, preferred_element_type=F32)
    hi = d(jnp.concatenate([a1, a2], axis=1), jnp.concatenate([b1, b1], axis=0))
    return hi + d(a1, b2)


def _seq_conv3(up, halo, cw, row, pos, tm, t):
    prev = jnp.where(row == 0, halo[7:8], pltpu.roll(up, 1, 0))
    prev = jnp.where(pos == 0, 0.0, prev)
    nxt = jnp.where(row == tm - 1, halo[8:9], pltpu.roll(up, tm - 1, 0))
    nxt = jnp.where(pos == t - 1, 0.0, nxt)
    return prev * cw[0:1] + up * cw[1:2] + nxt * cw[2:3]


def _ada_kernel(c_ref, w_ref, b_ref, o_ref):
    s = _silu(c_ref[...])
    o_ref[...] = _mm(s, w_ref[...]) + b_ref[...]


def _ada(cond8, w_ada, b_ada, tn=512):
    d, n = w_ada.shape
    return pl.pallas_call(
        _ada_kernel,
        out_shape=jax.ShapeDtypeStruct((8, n), F32),
        grid=(n // tn,),
        in_specs=[pl.BlockSpec((8, d), lambda j: (0, 0)),
                  pl.BlockSpec((d, tn), lambda j: (0, j)),
                  pl.BlockSpec((1, tn), lambda j: (0, j))],
        out_specs=pl.BlockSpec((8, tn), lambda j: (0, j)),
        compiler_params=_cparams(("arbitrary",)),
        name="ada_mod",
    )(cond8, w_ada, b_ada.reshape(1, n))


def _in_proj_kernel(x_ref, xp_ref, xn_ref, g_ref, sh_ref, sc_ref, w_ref, wx_ref, cw_ref,
                    o_ref, ox_ref, h_scr, hh_scr, *, t, tm, tn):
    i = pl.program_id(0)
    j = pl.program_id(1)
    n_qk = 2 * GDN_HEADS * GDN_DK // tn
    n_q = GDN_HEADS * GDN_DK // tn
    n_conv = GDN_QKV_W // tn

    def normmod(x):
        return _rms(x, g_ref[...]) * (1.0 + sc_ref[...]) + sh_ref[...]

    @pl.when(j == 0)
    def _():
        hb = normmod(x_ref[...]).astype(BF16)
        h_scr[...] = hb
        hh_scr[...] = jnp.concatenate([normmod(xp_ref[...]), normmod(xn_ref[...])], axis=0).astype(BF16)
        ox_ref[...] = jnp.dot(hb, wx_ref[...], preferred_element_type=F32)

    strips = [slice(cs * MXU_N, (cs + 1) * MXU_N) for cs in range(tn // MXU_N)]

    def conv_strip(cols):
        up = jnp.dot(h_scr[...], w_ref[:, cols], preferred_element_type=F32)
        halo = jnp.dot(hh_scr[...], w_ref[:, cols], preferred_element_type=F32)
        row = lax.broadcasted_iota(jnp.int32, (tm, 1), 0)
        pos = (i * tm + row) % t
        return _silu(_seq_conv3(up, halo, cw_ref[:, cols], row, pos, tm, t))

    @pl.when(j < n_qk)
    def _():
        qscale = jnp.where(j < n_q, GDN_DK ** -0.5, 1.0)
        for cols in strips:
            y = conv_strip(cols)
            for hd in range(MXU_N // GDN_DK):
                sl = slice(hd * GDN_DK, (hd + 1) * GDN_DK)
                o_ref[:, cols.start + hd * GDN_DK:cols.start + (hd + 1) * GDN_DK] = _l2n(y[:, sl]) * qscale

    @pl.when((j >= n_qk) & (j < n_conv))
    def _():
        for cols in strips:
            o_ref[:, cols] = conv_strip(cols)

    @pl.when(j >= n_conv)
    def _():
        o_ref[...] = jnp.dot(h_scr[...], w_ref[...], preferred_element_type=F32)


def _in_proj(x, norm_g, shift, scale, w_main, w_misc, conv_w, t, tm=1024, tn=512):
    m, d = x.shape
    n = w_main.shape[1]
    nx = w_misc.shape[1]
    tiles_per_group = (m // shift.shape[0]) // tm
    r8 = tm // 8
    last8 = m // 8 - 1
    last_conv = GDN_QKV_W // tn - 1

    def grp(i, j):
        return (i // tiles_per_group, 0, 0)

    return pl.pallas_call(
        functools.partial(_in_proj_kernel, t=t, tm=tm, tn=tn),
        out_shape=[jax.ShapeDtypeStruct((m, n), F32), jax.ShapeDtypeStruct((m, nx), F32)],
        grid=(m // tm, n // tn),
        in_specs=[pl.BlockSpec((tm, d), lambda i, j: (i, 0)),
                  pl.BlockSpec((8, d), lambda i, j: (jnp.maximum(i * r8 - 1, 0), 0)),
                  pl.BlockSpec((8, d), lambda i, j: (jnp.minimum((i + 1) * r8, last8), 0)),
                  pl.BlockSpec((1, d), lambda i, j: (0, 0)),
                  pl.BlockSpec((None, 1, d), grp),
                  pl.BlockSpec((None, 1, d), grp),
                  pl.BlockSpec((d, tn), lambda i, j: (0, j)),
                  pl.BlockSpec((d, nx), lambda i, j: (0, 0)),
                  pl.BlockSpec((3, tn), lambda i, j: (0, jnp.minimum(j, last_conv)))],
        out_specs=[pl.BlockSpec((tm, tn), lambda i, j: (i, j)),
                   pl.BlockSpec((tm, nx), lambda i, j: (i, 0))],
        scratch_shapes=[pltpu.VMEM((tm, d), BF16), pltpu.VMEM((16, d), BF16)],
        compiler_params=_cparams(("parallel", "arbitrary")),
        name="in_proj",
    )(x, x, x, norm_g.reshape(1, d), shift, scale, w_main, w_misc, conv_w)


def _norm_mm_kernel(*refs, normalize, emit_h):
    it = iter(refs)
    x_ref = next(it)
    g_ref = next(it) if normalize else None
    w_ref = next(it)
    o_ref = next(it)
    hn_ref = next(it) if emit_h else None
    h_scr = next(it)

    @pl.when(pl.program_id(1) == 0)
    def _():
        h = x_ref[...]
        if normalize:
            h = _rms(h, g_ref[...])
        if emit_h:
            hn_ref[...] = h
        h_scr[...] = h.astype(BF16)

    o_ref[...] = jnp.dot(h_scr[...], w_ref[...], preferred_element_type=F32).astype(o_ref.dtype)


def _norm_mm(x, w, *, xcol=0, k=None, g=None, emit_h=False, out_dtype=F32, tm=512, tn=512,
             name="norm_mm"):
    m = x.shape[0]
    k = x.shape[1] if k is None else k
    n = w.shape[1]
    tn = min(tn, n)
    tm = min(tm, m)
    normalize = g is not None
    ins = [x]
    in_specs = [pl.BlockSpec((tm, k), lambda i, j: (i, xcol))]
    if normalize:
        ins.append(g.reshape(1, k))
        in_specs.append(pl.BlockSpec((1, k), lambda i, j: (0, 0)))
    ins.append(w)
    in_specs.append(pl.BlockSpec((k, tn), lambda i, j: (0, j)))
    out_shape = [jax.ShapeDtypeStruct((m, n), out_dtype)]
    out_specs = [pl.BlockSpec((tm, tn), lambda i, j: (i, j))]
    if emit_h:
        out_shape.append(jax.ShapeDtypeStruct((m, k), F32))
        out_specs.append(pl.BlockSpec((tm, k), lambda i, j: (i, 0)))
    return pl.pallas_call(
        functools.partial(_norm_mm_kernel, normalize=normalize, emit_h=emit_h),
        out_shape=out_shape,
        grid=(m // tm, n // tn),
        in_specs=in_specs,
        out_specs=out_specs,
        scratch_shapes=[pltpu.VMEM((tm, k), BF16)],
        compiler_params=_cparams(("parallel", "arbitrary")),
        name=name,
    )(*ins)


def _gdn_kernel(*refs, cb, nb, hb, has_s0):
    (qf_ref, kf_ref, vf_ref, mf_ref, qb_ref, kb_ref, vb_ref, mb_ref, alog_ref, dtb_ref) = refs[:10]
    rest = refs[10:]
    if has_s0:
        s0_ref, rest = rest[0], rest[1:]
    of_ref, ob_ref, sfin_ref, st_scr = rest
    c = GDN_CHUNK
    tb = pl.program_id(2)
    head0 = pl.program_id(1) * hb

    @pl.when(tb == 0)
    def _():
        if has_s0:
            for hi in range(hb):
                st_scr[hi] = s0_ref[:, hi]
        else:
            st_scr[...] = jnp.zeros_like(st_scr)

    ii = lax.broadcasted_iota(jnp.int32, (c, c), 0)
    jj = lax.broadcasted_iota(jnp.int32, (c, c), 1)
    eye = jnp.where(ii == jj, 1.0, 0.0)
    masks = ((ii >= jj, ii > jj), (ii <= jj, ii < jj))

    chains = [(hi, d, ci) for hi in range(hb) for d in (0, 1) for ci in range(cb)]
    srcs = ((qf_ref, kf_ref, vf_ref, mf_ref), (qb_ref, kb_ref, vb_ref, mb_ref))

    def load(ch):
        hi, d, ci = ch
        q_ref, k_ref, v_ref, m_ref = srcs[d]
        rows = slice(ci * c, (ci + 1) * c)
        cols = slice(hi * c, (hi + 1) * c)
        mc = m_ref[rows, :]
        head = head0 + hi
        a_col = jnp.sum(jnp.where(jj == MISC_A + d * GDN_HEADS + head, mc, 0.0), axis=1, keepdims=True)
        b_col = jnp.sum(jnp.where(jj == MISC_B + d * GDN_HEADS + head, mc, 0.0), axis=1, keepdims=True)
        coef = -jnp.exp(alog_ref[hi, d:d + 1, :])
        g = coef * _softplus(a_col + dtb_ref[hi, d:d + 1, :])
        beta = 1.0 / (1.0 + jnp.exp(-b_col))
        for s in (1, 2, 4, 8, 16, 32, 64):
            if d == 0:
                g = g + jnp.where(ii >= s, pltpu.roll(g, s, 0), 0.0)
            else:
                g = g + jnp.where(ii < c - s, pltpu.roll(g, c - s, 0), 0.0)
        return q_ref[rows, cols], k_ref[rows, cols], v_ref[rows, cols], g, beta

    data = [load(ch) for ch in chains]
    decay = [jnp.exp(jnp.where(masks[ch[1]][0], g - g.T, -jnp.inf))
             for ch, (_, _, _, g, _) in zip(chains, data)]
    kbeta = [k * beta for (_, k, _, _, beta) in data]
    a_mat = [jnp.where(masks[ch[1]][1], _mm_nt(kb, k) * dc, 0.0)
             for ch, kb, (_, k, _, _, _), dc in zip(chains, kbeta, data, decay)]
    intra = [jnp.where(masks[ch[1]][0], _mm_nt(q, k) * dc, 0.0).astype(BF16)
             for ch, (q, k, _, _, _), dc in zip(chains, data, decay)]
    def in_blocks(s, x):
        k = s.bit_length() - 1
        return ((ii >> k) == (jj >> k)) if x else ((ii >> k) != (jj >> k))

    t_mat = [eye - jnp.where(in_blocks(2, True) & in_blocks(1, False), a, 0.0) for a in a_mat]
    s = 2
    while s < c:
        off = in_blocks(2 * s, True) & in_blocks(s, False)
        y = [_mm_x3(jnp.where(off, a, 0.0), x) for a, x in zip(a_mat, t_mat)]
        t_mat = [x - _mm_x3(x, y_) for x, y_ in zip(t_mat, y)]
        s *= 2
    eg = [jnp.exp(g) for (_, _, _, g, _) in data]
    u = [_mm(tm_, v * beta) for tm_, (_, _, v, _, beta) in zip(t_mat, data)]
    w = [_mm(tm_, kb * e) for tm_, kb, e in zip(t_mat, kbeta, eg)]
    w_qg = [jnp.concatenate([w_.astype(BF16), (q * e).astype(BF16)], axis=0)
            for w_, (q, _, _, _, _), e in zip(w, data, eg)]
    g_last = [(g[c - 1:c, :] if ch[1] == 0 else g[0:1, :]) for ch, (_, _, _, g, _) in zip(chains, data)]
    kg = [(k * jnp.exp(gl - g)).astype(BF16) for (_, k, _, g, _), gl in zip(data, g_last)]
    e_last = [jnp.exp(gl) for gl in g_last]

    idx = {ch: n for n, ch in enumerate(chains)}
    for hi in range(hb):
        for d, out_ref in ((0, of_ref), (1, ob_ref)):
            state = st_scr[hi, d]
            for step in range(cb):
                ci = step if d == 0 else cb - 1 - step
                n = idx[(hi, d, ci)]
                ws = jnp.dot(w_qg[n], state.astype(BF16), preferred_element_type=F32)
                v_new = u[n] - ws[:c]
                o = ws[c:] + jnp.dot(intra[n], v_new.astype(BF16), preferred_element_type=F32)
                out_ref[ci * c:(ci + 1) * c, hi * c:(hi + 1) * c] = o
                state = state * e_last[n] + _mm_tn(kg[n], v_new)
            st_scr[hi, d] = state

    @pl.when(tb == nb - 1)
    def _():
        for hi in range(hb):
            sfin_ref[:, hi] = st_scr[hi]


def _gdn(proj, misc, alog_b, dtb_b, s0, bsz, t, cb, hb):
    h = GDN_HEADS
    tblk = cb * GDN_CHUNK
    nb = t // tblk
    hw = hb * LANES
    has_s0 = s0 is not None

    def fwd(off):
        return pl.BlockSpec((None, tblk, hw), lambda b, hg, tb: (b, tb, off + hg))

    def bwd(off):
        return pl.BlockSpec((None, tblk, hw), lambda b, hg, tb: (b, nb - 1 - tb, off + hg))

    hg_n = h // hb
    in_specs = [fwd(0), fwd(hg_n), fwd(2 * hg_n),
                pl.BlockSpec((None, tblk, LANES), lambda b, hg, tb: (b, tb, 0)),
                bwd(0), bwd(hg_n), bwd(2 * hg_n),
                pl.BlockSpec((None, tblk, LANES), lambda b, hg, tb: (b, nb - 1 - tb, 0)),
                pl.BlockSpec((hb, 2, LANES), lambda b, hg, tb: (hg, 0, 0)),
                pl.BlockSpec((hb, 2, LANES), lambda b, hg, tb: (hg, 0, 0))]
    ins = [proj, proj, proj, misc, proj, proj, proj, misc, alog_b, dtb_b]
    st_spec = pl.BlockSpec((None, 2, hb, GDN_DK, GDN_DV), lambda b, hg, tb: (b, 0, hg, 0, 0))
    if has_s0:
        in_specs.append(st_spec)
        ins.append(s0)
    return pl.pallas_call(
        functools.partial(_gdn_kernel, cb=cb, nb=nb, hb=hb, has_s0=has_s0),
        out_shape=[jax.ShapeDtypeStruct((bsz, t, GDN_OUT_W), F32),
                   jax.ShapeDtypeStruct((bsz, t, GDN_OUT_W), F32),
                   jax.ShapeDtypeStruct((bsz, 2, h, GDN_DK, GDN_DV), F32)],
        grid=(bsz, hg_n, nb),
        in_specs=in_specs,
        out_specs=[pl.BlockSpec((None, tblk, hw), lambda b, hg, tb: (b, tb, hg)),
                   pl.BlockSpec((None, tblk, hw), lambda b, hg, tb: (b, nb - 1 - tb, hg)),
                   st_spec],
        scratch_shapes=[pltpu.VMEM((hb, 2, GDN_DK, GDN_DV), F32)],
        compiler_params=_cparams(("parallel", "parallel", "arbitrary")),
        name="gdn",
    )(*ins)


def _rope(x, cos, sin):
    half = MLA_ROPE // 2
    x1, x2 = x[:, :half], x[:, half:]
    return jnp.concatenate([x1 * cos - x2 * sin, x1 * sin + x2 * cos], axis=1)


def _attn_kernel(*refs, rope, two_sets, s1, s2, tsub):
    it = iter(refs)
    q_ref, kv1_ref, m1_ref = next(it), next(it), next(it)
    kv2_ref = next(it) if two_sets else None
    kp2_ref = next(it) if two_sets else None
    if rope:
        cq_ref, sq_ref, ck_ref, sk_ref = next(it), next(it), next(it), next(it)
    o_ref, kfull = next(it), next(it)

    @pl.when(pl.program_id(2) == 0)
    def _():
        kr = m1_ref[:, MISC_KPE:MISC_KPE + MLA_ROPE]
        if rope:
            kr = _rope(kr, ck_ref[...], sk_ref[...])
        kfull[0:s1, 0:LANES] = kv1_ref[:, 0:MLA_NOPE]
        kfull[0:s1, LANES:2 * LANES] = jnp.concatenate(
            [kr, jnp.zeros((s1, LANES - MLA_ROPE), F32)], axis=1).astype(BF16)
        if two_sets:
            kfull[s1:s1 + s2, 0:LANES] = kv2_ref[:, 0:MLA_NOPE]
            kfull[s1:s1 + s2, LANES:2 * LANES] = jnp.concatenate(
                [kp2_ref[...], jnp.zeros((s2, LANES - MLA_ROPE), F32)], axis=1).astype(BF16)

    for r0 in range(0, q_ref.shape[0], tsub):
        rows = slice(r0, r0 + tsub)
        if rope:
            q = q_ref[rows, :]
            qr = _rope(q[:, MLA_NOPE:MLA_QK], cq_ref[rows, :], sq_ref[rows, :])
            qf = jnp.concatenate([q[:, :MLA_NOPE], qr, jnp.zeros((tsub, LANES - MLA_ROPE), F32)],
                                 axis=1).astype(BF16)
        else:
            qf = q_ref[rows, :]
        s = _mm_nt(qf, kfull[...]) * (MLA_QK ** -0.5)
        m = _row_max(s)
        p = jnp.exp(s - m)
        l = _row_sum(p)
        pb = p.astype(BF16)
        o = jnp.dot(pb[:, :s1], kv1_ref[:, MLA_NOPE:], preferred_element_type=F32)
        if two_sets:
            o = o + jnp.dot(pb[:, s1:], kv2_ref[:, MLA_NOPE:], preferred_element_type=F32)
        o_ref[rows, :] = (o / l).astype(o_ref.dtype)


def _attn(q, kv1, misc1, kv2=None, kpe2=None, rope_tabs=None, tq=512, tsub=256):
    bsz, t, _ = q.shape
    s1 = kv1.shape[1]
    two_sets = kv2 is not None
    s2 = kv2.shape[1] if two_sets else 0
    rope = rope_tabs is not None
    tq = min(tq, t)
    hw = MLA_QPAD
    ins = [q, kv1, misc1]
    in_specs = [pl.BlockSpec((None, tq, hw), lambda b, h, i: (b, i, h)),
                pl.BlockSpec((None, s1, hw), lambda b, h, i: (b, 0, h)),
                pl.BlockSpec((None, s1, LANES), lambda b, h, i: (b, 0, 0))]
    if two_sets:
        ins += [kv2, kpe2]
        in_specs += [pl.BlockSpec((None, s2, hw), lambda b, h, i: (b, 0, h)),
                     pl.BlockSpec((None, s2, MLA_ROPE), lambda b, h, i: (b, 0, 0))]
    if rope:
        cos, sin = rope_tabs
        half = MLA_ROPE // 2
        ins += [cos, sin, cos, sin]
        in_specs += [pl.BlockSpec((tq, half), lambda b, h, i: (i, 0)),
                     pl.BlockSpec((tq, half), lambda b, h, i: (i, 0)),
                     pl.BlockSpec((s1, half), lambda b, h, i: (0, 0)),
                     pl.BlockSpec((s1, half), lambda b, h, i: (0, 0))]
    return pl.pallas_call(
        functools.partial(_attn_kernel, rope=rope, two_sets=two_sets, s1=s1, s2=s2, tsub=min(tsub, tq)),
        out_shape=jax.ShapeDtypeStruct((bsz, t, MLA_OUT_W), BF16),
        grid=(bsz, MLA_HEADS, t // tq),
        in_specs=in_specs,
        out_specs=pl.BlockSpec((None, tq, MLA_DV), lambda b, h, i: (b, i, h)),
        scratch_shapes=[pltpu.VMEM((s1 + s2, hw), BF16)],
        compiler_params=_cparams(("parallel", "parallel", "arbitrary")),
        name="mla_attn",
    )(*ins)


def _outproj_kernel(of_ref, ob_ref, z_ref, ng_ref, a2_ref, w1_ref, w2_ref, x_ref, gate_ref,
                    o_ref, a1_scr):
    @pl.when(pl.program_id(1) == 0)
    def _():
        for hd in range(GDN_HEADS):
            sl = slice(hd * GDN_DV, (hd + 1) * GDN_DV)
            o = _rms(of_ref[:, sl] + ob_ref[:, sl], ng_ref[...]) * _silu(z_ref[:, sl])
            a1_scr[:, sl] = o.astype(BF16)

    acc = jnp.dot(a1_scr[...], w1_ref[...], preferred_element_type=F32)
    acc = acc + jnp.dot(a2_ref[...], w2_ref[...], preferred_element_type=F32)
    o_ref[...] = x_ref[...] + gate_ref[...] * acc


def _outproj(o_f, o_b, proj, norm_g, a2, w, x, gate, tm=512, tn=512):
    m, k1 = o_f.shape
    k2 = a2.shape[1]
    n = w.shape[1]
    tiles_per_group = (m // gate.shape[0]) // tm
    return pl.pallas_call(
        _outproj_kernel,
        out_shape=jax.ShapeDtypeStruct((m, n), F32),
        grid=(m // tm, n // tn),
        in_specs=[pl.BlockSpec((tm, k1), lambda i, j: (i, 0)),
                  pl.BlockSpec((tm, k1), lambda i, j: (i, 0)),
                  pl.BlockSpec((tm, k1), lambda i, j: (i, Z_COL0 // k1)),
                  pl.BlockSpec((1, GDN_DV), lambda i, j: (0, 0)),
                  pl.BlockSpec((tm, k2), lambda i, j: (i, 0)),
                  pl.BlockSpec((k1, tn), lambda i, j: (0, j)),
                  pl.BlockSpec((k2, tn), lambda i, j: (k1 // k2, j)),
                  pl.BlockSpec((tm, tn), lambda i, j: (i, j)),
                  pl.BlockSpec((None, 1, tn), lambda i, j: (i // tiles_per_group, 0, j))],
        out_specs=pl.BlockSpec((tm, tn), lambda i, j: (i, j)),
        scratch_shapes=[pltpu.VMEM((tm, k1), BF16)],
        compiler_params=_cparams(("parallel", "arbitrary")),
        name="out_proj",
    )(o_f, o_b, proj, norm_g.reshape(1, GDN_DV), a2, w, w, x, gate)


def _ffn_kernel(x_ref, xp_ref, xn_ref, g_ref, sh_ref, sc_ref, gate_ref, wa_ref, wg_ref,
                cwa_ref, cwg_ref, cba_ref, cbg_ref, wd_ref, fg_ref, o_ref,
                h_scr, hh_scr, *, t, tm):
    i = pl.program_id(0)
    j = pl.program_id(1)

    def normmod(x):
        return _rms(x, g_ref[...]) * (1.0 + sc_ref[...]) + sh_ref[...]

    @pl.when(j == 0)
    def _():
        h_scr[...] = normmod(x_ref[...]).astype(BF16)
        hh_scr[...] = jnp.concatenate([normmod(xp_ref[...]), normmod(xn_ref[...])], axis=0).astype(BF16)
        o_ref[...] = jnp.zeros_like(o_ref)

    h = h_scr[...]
    hh = hh_scr[...]
    row = lax.broadcasted_iota(jnp.int32, (tm, 1), 0)
    pos = (i * tm + row) % t

    def conv_half(w_ref, cw_ref, cb_ref, cols):
        up = jnp.dot(h, w_ref[:, cols], preferred_element_type=F32)
        halo = jnp.dot(hh, w_ref[:, cols], preferred_element_type=F32)
        return _seq_conv3(up, halo, cw_ref[:, cols], row, pos, tm, t) + cb_ref[:, cols]

    down = None
    for cs in range(wd_ref.shape[0] // MXU_N):
        cols = slice(cs * MXU_N, (cs + 1) * MXU_N)
        a = conv_half(wa_ref, cwa_ref, cba_ref, cols)
        gt = conv_half(wg_ref, cwg_ref, cbg_ref, cols)
        act = (_silu(a) * gt).astype(BF16)
        part = jnp.dot(act, wd_ref[cols, :], preferred_element_type=F32)
        down = part if down is None else down + part
    o_ref[...] += down

    @pl.when(j == pl.num_programs(1) - 1)
    def _():
        x2 = x_ref[...] + gate_ref[...] * o_ref[...]
        o_ref[...] = _rms(x2, fg_ref[...])


def _ffn(x, norm_g, shift, scale, gate, w_up, conv_w, conv_b, w_down, final_g, t, tm=1024, tn=256):
    m, d = x.shape
    ff = w_down.shape[0]
    nj = ff // tn
    tiles_per_group = (m // shift.shape[0]) // tm
    r8 = tm // 8
    last8 = m // 8 - 1

    def grp(i, j):
        return (i // tiles_per_group, 0, 0)

    row_vec = lambda n: pl.BlockSpec((1, n), lambda i, j: (0, 0))
    return pl.pallas_call(
        functools.partial(_ffn_kernel, t=t, tm=tm),
        out_shape=jax.ShapeDtypeStruct((m, d), F32),
        grid=(m // tm, nj),
        in_specs=[pl.BlockSpec((tm, d), lambda i, j: (i, 0), pipeline_mode=pl.Buffered(1)),
                  pl.BlockSpec((8, d), lambda i, j: (jnp.maximum(i * r8 - 1, 0), 0)),
                  pl.BlockSpec((8, d), lambda i, j: (jnp.minimum((i + 1) * r8, last8), 0)),
                  row_vec(d),
                  pl.BlockSpec((None, 1, d), grp),
                  pl.BlockSpec((None, 1, d), grp),
                  pl.BlockSpec((None, 1, d), grp),
                  pl.BlockSpec((d, tn), lambda i, j: (0, j)),
                  pl.BlockSpec((d, tn), lambda i, j: (0, nj + j)),
                  pl.BlockSpec((3, tn), lambda i, j: (0, j)),
                  pl.BlockSpec((3, tn), lambda i, j: (0, nj + j)),
                  pl.BlockSpec((1, tn), lambda i, j: (0, j)),
                  pl.BlockSpec((1, tn), lambda i, j: (0, nj + j)),
                  pl.BlockSpec((tn, d), lambda i, j: (j, 0)),
                  row_vec(d)],
        out_specs=pl.BlockSpec((tm, d), lambda i, j: (i, 0)),
        scratch_shapes=[pltpu.VMEM((tm, d), BF16), pltpu.VMEM((16, d), BF16)],
        compiler_params=_cparams(("parallel", "arbitrary")),
        name="conv_ffn",
    )(x, x, x, norm_g.reshape(1, d), shift, scale, gate, w_up, w_up, conv_w, conv_w,
      conv_b.reshape(1, -1), conv_b.reshape(1, -1), w_down, final_g.reshape(1, d))


def _axial_rope_tables(t):
    rows = t // GRID_W
    row = jnp.repeat(jnp.arange(rows, dtype=F32), GRID_W)
    col = jnp.tile(jnp.arange(GRID_W, dtype=F32), rows)
    n_freq = MLA_ROPE // 4
    inv_freq = ROPE_THETA ** (-jnp.arange(n_freq, dtype=F32) / n_freq)
    ang = jnp.concatenate([row[:, None] * inv_freq, col[:, None] * inv_freq], axis=-1)
    return jnp.cos(ang), jnp.sin(ang)


def _gdn_blocking(t):
    cb = min(4, t // GDN_CHUNK)
    return cb, max(1, 4 // cb)


def _trunk(x, mod, p, ctx):
    bsz, t, d = x.shape
    m = bsz * t
    x2d = x.reshape(m, d)
    mods = [mod[:, i].reshape(-1, 1, d) for i in range(N_MOD)]
    shift1, scale1, gate1, shift2, scale2, gate2 = mods

    proj, misc = _in_proj(x2d, p['norm1_g'], shift1, scale1, p['w_in_main'], p['w_in_misc'],
                          p['gdn_conv_w'], t)
    s0 = None if ctx is None else ctx[0]
    cb, hb = _gdn_blocking(t)
    o_f, o_b, s_fin = _gdn(proj.reshape(bsz, t, MAIN_W), misc.reshape(bsz, t, LANES),
                           p['alog_b'], p['dtb_b'], s0, bsz, t, cb, hb)

    hq = MLA_HEADS * MLA_QPAD
    q = _norm_mm(proj, p['w_q_b'], xcol=QA_COL0 // MLA_Q_LORA, k=MLA_Q_LORA, g=p['mla_q_norm_g'],
                 out_dtype=BF16 if ctx is None else F32, tn=hq, name="q_proj")[0]
    kv, ckv = _norm_mm(proj, p['w_kv_b'], xcol=KVA_COL0 // MLA_KV_LORA, k=MLA_KV_LORA,
                       g=p['mla_kv_norm_g'], emit_h=True, out_dtype=BF16, tn=hq, name="kv_proj")
    if ctx is None:
        mla_o = _attn(q.reshape(bsz, t, hq), kv.reshape(bsz, t, hq), misc.reshape(bsz, t, LANES))
    else:
        past = ctx[1].shape[1]
        kv_c = _norm_mm(ctx[1].reshape(bsz * past, MLA_KV_LORA), p['w_kv_b'], out_dtype=BF16,
                        tn=hq, name="kv_proj_cache")[0]
        mla_o = _attn(q.reshape(bsz, t, hq), kv.reshape(bsz, t, hq), misc.reshape(bsz, t, LANES),
                      kv_c.reshape(bsz, past, hq), ctx[2], rope_tabs=_axial_rope_tables(t))

    x1 = _outproj(o_f.reshape(m, GDN_OUT_W), o_b.reshape(m, GDN_OUT_W), proj, p['gdn_norm_g'],
                  mla_o.reshape(m, MLA_OUT_W), p['w_out'], x2d, gate1)
    y = _ffn(x1, p['norm2_g'], shift2, scale2, gate2, p['w_up'], p['ffn_conv_w'], p['ffn_conv_b'],
             p['w_down'], p['final_norm_g'], t)
    y = y.reshape(bsz, t, d)
    if ctx is None:
        kpe = misc[:, MISC_KPE:MISC_KPE + MLA_ROPE].reshape(bsz, t, MLA_ROPE)
        return y, s_fin, ckv.reshape(bsz, t, MLA_KV_LORA), kpe
    return y


def _prep_params(w_in, gdn_conv_w, gdn_a_log, gdn_dt_bias, gdn_norm_g, mla_q_norm_g, mla_w_q_b,
                 mla_kv_norm_g, mla_w_kv_b, w_out, norm1_g, norm2_g, w_up, ffn_conv_w, ffn_conv_b,
                 w_down, final_norm_g):
    d = D_MODEL
    wi = w_in[0]
    a0 = GDN_QKV_W + GDN_OUT_W
    qa0 = a0 + 4 * GDN_HEADS
    kpe0 = qa0 + MLA_Q_LORA + MLA_KV_LORA
    w_in_main = jnp.concatenate([wi[:, :a0], wi[:, qa0:kpe0]], axis=1).astype(BF16)
    w_in_misc = jnp.concatenate([wi[:, kpe0:], wi[:, a0:qa0],
                                 jnp.zeros((d, LANES - MLA_ROPE - 4 * GDN_HEADS), F32)], axis=1).astype(BF16)
    w_q_b = jnp.pad(mla_w_q_b[0].reshape(MLA_Q_LORA, MLA_HEADS, MLA_QK),
                    ((0, 0), (0, 0), (0, MLA_QPAD - MLA_QK))).reshape(MLA_Q_LORA, -1).astype(BF16)

    def lane_bcast(v):
        return jnp.broadcast_to(v.T[:, :, None], (GDN_HEADS, 2, LANES)).astype(F32)

    return {
        'w_in_main': w_in_main, 'w_in_misc': w_in_misc, 'norm1_g': norm1_g[0],
        'gdn_conv_w': gdn_conv_w[0], 'alog_b': lane_bcast(gdn_a_log[0]), 'dtb_b': lane_bcast(gdn_dt_bias[0]),
        'gdn_norm_g': gdn_norm_g[0], 'mla_q_norm_g': mla_q_norm_g[0], 'w_q_b': w_q_b,
        'mla_kv_norm_g': mla_kv_norm_g[0], 'w_kv_b': mla_w_kv_b[0].astype(BF16),
        'w_out': w_out[0].astype(BF16), 'norm2_g': norm2_g[0], 'w_up': w_up[0].astype(BF16),
        'ffn_conv_w': ffn_conv_w[0], 'ffn_conv_b': ffn_conv_b[0], 'w_down': w_down[0].astype(BF16),
        'final_norm_g': final_norm_g,
    }


def kernel(x_prompt, x_sample, state_gdn, cache_mla_ckv, cache_mla_kpe, c, c_ctx, w_ada, b_ada, norm1_g, w_in, gdn_conv_w, gdn_a_log, gdn_dt_bias, gdn_norm_g, mla_q_norm_g, mla_w_q_b, mla_kv_norm_g, mla_w_kv_b, w_out, norm2_g, w_up, ffn_conv_w, ffn_conv_b, w_down, final_norm_g):
    assert w_in.shape[0] == 1, "single-layer trunk"
    d = D_MODEL
    dec_b = x_sample.shape[0]
    assert 1 + dec_b <= 8

    cond8 = jnp.zeros((8, d), F32).at[0].set(c_ctx).at[1:1 + dec_b].set(c)
    mod = _ada(cond8, w_ada[0], b_ada[0]).reshape(8, N_MOD, d)
    p = _prep_params(w_in, gdn_conv_w, gdn_a_log, gdn_dt_bias, gdn_norm_g, mla_q_norm_g, mla_w_q_b,
                     mla_kv_norm_g, mla_w_kv_b, w_out, norm1_g, norm2_g, w_up, ffn_conv_w, ffn_conv_b,
                     w_down, final_norm_g)

    y_prompt, s_fin, ckv, kpe = _trunk(x_prompt, mod[0:1], p, None)
    y_sample = _trunk(x_sample, mod[1:1 + dec_b], p,
                      (state_gdn[:, 0], cache_mla_ckv[:, 0], cache_mla_kpe[:, 0]))
    return (y_prompt, y_sample, s_fin[:, None], ckv[:, None], kpe[:, None])
```
